```python
import math
import jax, jax.numpy as jnp
from jax import lax
import numpy as np

D_MODEL = 1024
BATCH = 8
SEQ = 2048
DEPTH = 4
DEC_BATCH = 128
DEC_SEQ = 8
PAST_LEN = 16384
PAGE_SIZE = 128

N_MEM = 256
X_HEADS = 4
X_DH = D_MODEL // X_HEADS
M_HEADS = 4
M_DH = D_MODEL // M_HEADS
M_WIDTH = M_HEADS * M_DH
R_WIDTH = D_MODEL
R_BLOCKS = 8
R_BDIM = R_WIDTH // R_BLOCKS
R_C = 8.0
G_HEADS = 8
G_DH = D_MODEL // G_HEADS
G_WIDTH = G_HEADS * G_DH
BR_WIDTH = D_MODEL
N_BRANCH = 3
CONV_W = 4
CHUNK = 64
D_FF = 2816
EPS = 1e-6
M_INIT = -1e30

CONV_CH = R_WIDTH + 3 * G_WIDTH
OFF_RX = 0
OFF_GQ = OFF_RX + R_WIDTH
OFF_GK = OFF_GQ + G_WIDTH
OFF_GV = OFF_GK + G_WIDTH
OFF_MQ = OFF_GV + G_WIDTH
OFF_MK = OFF_MQ + M_WIDTH
OFF_MV = OFF_MK + M_WIDTH
OFF_MO = OFF_MV + M_WIDTH
OFF_RY = OFF_MO + M_WIDTH
OFF_GZ = OFF_RY + R_WIDTH
OFF_MI = OFF_GZ + G_WIDTH
OFF_MF = OFF_MI + M_HEADS
OFF_GB = OFF_MF + M_HEADS
OFF_GA = OFF_GB + G_HEADS
D_IN = OFF_GA + G_HEADS

kernel_name = 'hybrid_mlstm_rglru_gdn_step'


def rmsnorm(x, g):
    xf = x.astype(jnp.float32)
    y = xf * lax.rsqrt(jnp.mean(xf * xf, axis=-1, keepdims=True) + EPS)
    return (y * g.astype(jnp.float32)).astype(x.dtype)


def l2norm(x):
    return x * lax.rsqrt(jnp.sum(x * x, axis=-1, keepdims=True) + EPS)


def swiglu(x, w_gu, w_down):
    g, u = jnp.split(x @ w_gu, 2, axis=-1)
    return (jax.nn.silu(g) * u) @ w_down


def causal_conv(u, buf, w, b):
    L = u.shape[1]
    full = jnp.concatenate([buf.astype(u.dtype), u], axis=1)
    y = full[:, 0:L] * w[0]
    for j in range(1, CONV_W):
        y = y + full[:, j:j + L] * w[j]
    return y + b, full[:, -(CONV_W - 1):]


def to_chunks(t, c):
    B, L = t.shape[:2]
    return jnp.moveaxis(t.reshape((B, L // c, c) + t.shape[2:]), 1, 0)


def from_chunks(t):
    nc, B, c = t.shape[:3]
    return jnp.moveaxis(t, 0, 1).reshape((B, nc * c) + t.shape[3:])


def mlstm(q, k, v, logi, logf, C0, n0, m0):
    L, D = q.shape[1], q.shape[3]
    c = math.gcd(CHUNK, L)
    tri = jnp.tril(jnp.ones((c, c), bool))
    xs = tuple(to_chunks(t, c) for t in (q, k * D ** -0.5, v, logi, logf))

    def step(carry, inp):
        C, n, m = carry
        qc, kc, vc, li, lf = inp
        b = jnp.cumsum(lf, axis=1)
        logD = b[:, :, None, :] - b[:, None, :, :] + li[:, None, :, :]
        logD = jnp.where(tri[None, :, :, None], logD, -jnp.inf)
        inter = b + m[:, None, :]
        mt = jnp.maximum(inter, jnp.max(logD, axis=2))
        wi = jnp.exp(inter - mt)
        s = jnp.einsum('bthd,bshd->btsh', qc, kc) * jnp.exp(logD - mt[:, :, None, :])
        num = jnp.einsum('btsh,bshd->bthd', s, vc) + wi[..., None] * jnp.einsum('bthk,bhkv->bthv', qc, C)
        den = jnp.sum(s, axis=2) + wi * jnp.einsum('bthk,bhk->bth', qc, n)
        h = num / jnp.maximum(jnp.abs(den), jnp.exp(-mt))[..., None]
        m_new = mt[:, -1]
        w_s = jnp.exp(b[:, -1:, :] - b + li - m_new[:, None, :])
        dec = jnp.exp(b[:, -1] + m - m_new)
        C_new = dec[..., None, None] * C + jnp.einsum('bsh,bshk,bshv->bhkv', w_s, kc, vc)
        n_new = dec[..., None] * n + jnp.einsum('bsh,bshk->bhk', w_s, kc)
        return (C_new, n_new, m_new), h

    (C, n, m), hs = lax.scan(step, (C0, n0, m0), xs)
    return from_chunks(hs), C, n, m


def rg_lru_scan(a, bx, h0):
    def comb(l, r):
        return l[0] * r[0], r[0] * l[1] + r[1]
    A, Bc = lax.associative_scan(comb, (a, bx), axis=1)
    h = A * h0[:, None, :] + Bc
    return h, h[:, -1]


def gated_delta(q, k, v, g, beta, S0):
    L = q.shape[1]
    c = math.gcd(CHUNK, L)
    tri = jnp.tril(jnp.ones((c, c), bool))
    strict = jnp.tril(jnp.ones((c, c), bool), -1)
    eye = jnp.eye(c, dtype=jnp.float32)
    xs = tuple(to_chunks(t, c) for t in (q, k, v, g, beta))

    def step(S, inp):
        qc, kc, vc, gc, bc = inp
        G = jnp.cumsum(gc, axis=1)
        Gh = jnp.swapaxes(G, 1, 2)
        dec = jnp.exp(jnp.where(tri, Gh[..., :, None] - Gh[..., None, :], -jnp.inf))
        bh = jnp.swapaxes(bc, 1, 2)
        kk = jnp.einsum('bthd,bshd->bhts', kc, kc)
        Lm = jnp.where(strict, bh[..., :, None] * kk * dec, 0.0)
        T = lax.linalg.triangular_solve(eye + Lm, jnp.broadcast_to(eye, Lm.shape), left_side=True, lower=True)
        eG = jnp.exp(G)[..., None]
        u0 = jnp.einsum('bhts,bshd->bthd', T, vc * bc[..., None])
        wk = jnp.einsum('bhts,bshd->bthd', T, kc * bc[..., None] * eG)
        u = u0 - jnp.einsum('bthk,bhkv->bthv', wk, S)
        qk = jnp.einsum('bthd,bshd->bhts', qc, kc) * dec
        o = jnp.einsum('bthk,bhkv->bthv', qc * eG, S) + jnp.einsum('bhts,bshv->bthv', qk, u)
        gl = G[:, -1]
        S_new = jnp.exp(gl)[..., None, None] * S + jnp.einsum('bshk,bshv->bhkv', kc * jnp.exp(gl[:, None, :] - G)[..., None], u)
        return S_new, o

    S, os_ = lax.scan(step, S0, xs)
    return from_chunks(os_), S


def token_mixing(xn, conv0, C0, n0, m0, h0, S0, w_in, conv_w, conv_b, m_ib, m_fb, m_norm,
                 r_wa, r_ba, r_wx, r_bx, r_lam, g_alog, g_dtb, g_norm, w_gate, b_gate, w_branch, w_out):
    B, L, _ = xn.shape
    f32 = jnp.float32
    proj = xn @ w_in
    conv_out, conv_new = causal_conv(proj[..., :CONV_CH], conv0, conv_w, conv_b)
    conv_out = conv_out.astype(f32)

    mq = proj[..., OFF_MQ:OFF_MK].astype(f32).reshape(B, L, M_HEADS, M_DH)
    mk = proj[..., OFF_MK:OFF_MV].astype(f32).reshape(B, L, M_HEADS, M_DH)
    mv = proj[..., OFF_MV:OFF_MO].astype(f32).reshape(B, L, M_HEADS, M_DH)
    logi = proj[..., OFF_MI:OFF_MF].astype(f32) + m_ib
    logf = jax.nn.log_sigmoid(proj[..., OFF_MF:OFF_GB].astype(f32) + m_fb)
    hm, C1, n1, m1 = mlstm(mq, mk, mv, logi, logf, C0.astype(f32), n0.astype(f32), m0.astype(f32))
    hm = rmsnorm(hm, m_norm.reshape(M_HEADS, M_DH)).reshape(B, L, M_WIDTH)
    hm = hm * jax.nn.sigmoid(proj[..., OFF_MO:OFF_RY].astype(f32))

    rx = conv_out[..., OFF_RX:OFF_GQ]
    rxb = rx.reshape(B, L, R_BLOCKS, R_BDIM)
    r = jax.nn.sigmoid(jnp.einsum('blni,nij->blnj', rxb, r_wa).reshape(B, L, R_WIDTH) + r_ba)
    ig = jax.nn.sigmoid(jnp.einsum('blni,nij->blnj', rxb, r_wx).reshape(B, L, R_WIDTH) + r_bx)
    log_a = -R_C * r * jax.nn.softplus(-r_lam.astype(f32))
    bx = jnp.sqrt(-jnp.expm1(2.0 * log_a)) * (ig * rx)
    hr, h1 = rg_lru_scan(jnp.exp(log_a), bx, h0.astype(f32))
    hr = hr * jax.nn.gelu(proj[..., OFF_RY:OFF_GZ].astype(f32))

    gq = jax.nn.silu(conv_out[..., OFF_GQ:OFF_GK]).reshape(B, L, G_HEADS, G_DH)
    gk = jax.nn.silu(conv_out[..., OFF_GK:OFF_GV]).reshape(B, L, G_HEADS, G_DH)
    gv = jax.nn.silu(conv_out[..., OFF_GV:CONV_CH]).reshape(B, L, G_HEADS, G_DH)
    gq = l2norm(gq) * G_DH ** -0.5
    gk = l2norm(gk)
    beta = jax.nn.sigmoid(proj[..., OFF_GB:OFF_GA].astype(f32))
    gdec = -jnp.exp(g_alog.astype(f32)) * jax.nn.softplus(proj[..., OFF_GA:D_IN].astype(f32) + g_dtb)
    hd, S1 = gated_delta(gq, gk, gv, gdec, beta, S0.astype(f32))
    hd = rmsnorm(hd, g_norm) * jax.nn.silu(proj[..., OFF_GZ:OFF_MI].astype(f32).reshape(B, L, G_HEADS, G_DH))
    hd = hd.reshape(B, L, G_WIDTH)

    gates = jax.nn.sigmoid((xn @ w_gate).astype(f32) + b_gate).reshape(B, L, N_BRANCH, D_MODEL)
    branches = jnp.stack([hm, hr, hd], axis=2).astype(xn.dtype)
    up = jnp.einsum('blnw,nwd->blnd', branches, w_branch)
    merged = jnp.sum(gates * up.astype(f32), axis=2).astype(xn.dtype)
    return merged @ w_out, (conv_new, C1, n1, m1, h1, S1)


def cross_attention(xn, mem_k, mem_v, wq, wo):
    B, L, _ = xn.shape
    q = (xn @ wq).reshape(B, L, X_HEADS, X_DH)
    s = jnp.einsum('blhd,bmhd->bhlm', q, mem_k.astype(q.dtype)).astype(jnp.float32) * X_DH ** -0.5
    p = jax.nn.softmax(s, axis=-1).astype(xn.dtype)
    o = jnp.einsum('bhlm,bmhd->blhd', p, mem_v.astype(xn.dtype)).reshape(B, L, D_MODEL)
    return o @ wo


def decoder_layer(x, mem_k, mem_v, state, w):
    (f1_n, f1_gu, f1_d, mix_n, w_in, conv_w, conv_b, m_ib, m_fb, m_norm,
     r_wa, r_ba, r_wx, r_bx, r_lam, g_alog, g_dtb, g_norm,
     w_gate, b_gate, w_branch, w_out, xa_n, xa_wq, xa_wo, f2_n, f2_gu, f2_d) = w
    conv0, C0, n0, m0, h0, S0 = state
    x = x + 0.5 * swiglu(rmsnorm(x, f1_n), f1_gu, f1_d)
    mixed, new_state = token_mixing(rmsnorm(x, mix_n), conv0, C0, n0, m0, h0, S0, w_in, conv_w, conv_b,
                                    m_ib, m_fb, m_norm, r_wa, r_ba, r_wx, r_bx, r_lam, g_alog, g_dtb,
                                    g_norm, w_gate, b_gate, w_branch, w_out)
    x = x + mixed
    x = x + cross_attention(rmsnorm(x, xa_n), mem_k, mem_v, xa_wq, xa_wo)
    x = x + 0.5 * swiglu(rmsnorm(x, f2_n), f2_gu, f2_d)
    return x, new_state


def setup_inputs(seed: int = 0) -> dict:
    key = jax.random.key(seed)
    ks = iter(jax.random.split(key, 64))
    f32 = jnp.float32
    d = D_MODEL

    def nrm(shape, scale):
        return jax.random.normal(next(ks), shape, f32) * scale

    def gain(shape):
        return 1.0 + nrm(shape, 0.02)

    u = jax.random.uniform(next(ks), (DEPTH, R_WIDTH), f32, 0.9, 0.999)
    s_lam = u ** (1.0 / R_C)
    dt = jnp.exp(jax.random.uniform(next(ks), (DEPTH, G_HEADS), f32, math.log(1e-3), math.log(0.1)))
    return {
        'x_prompt': nrm((BATCH, SEQ, d), 1.0),
        'x_sample': nrm((DEC_BATCH, DEC_SEQ, d), 1.0),
        'cache_mem_k': nrm((DEPTH, DEC_BATCH, N_MEM, X_HEADS, X_DH), 1.0),
        'cache_mem_v': nrm((DEPTH, DEC_BATCH, N_MEM, X_HEADS, X_DH), 1.0),
        'state_conv': nrm((DEPTH, DEC_BATCH, CONV_W - 1, CONV_CH), 1.0),
        'state_mlstm_C': nrm((DEPTH, DEC_BATCH, M_HEADS, M_DH, M_DH), 0.05),
        'state_mlstm_n': nrm((DEPTH, DEC_BATCH, M_HEADS, M_DH), 0.1),
        'state_mlstm_m': nrm((DEPTH, DEC_BATCH, M_HEADS), 0.5),
        'state_rglru_h': nrm((DEPTH, DEC_BATCH, R_WIDTH), 0.5),
        'state_delta_S': nrm((DEPTH, DEC_BATCH, G_HEADS, G_DH, G_DH), 0.1),
        'mem_prompt': nrm((BATCH, N_MEM, d), 1.0),
        'ffn1_norm': gain((DEPTH, d)),
        'ffn1_w_gu': nrm((DEPTH, d, 2 * D_FF), d ** -0.5),
        'ffn1_w_down': nrm((DEPTH, D_FF, d), D_FF ** -0.5),
        'mix_norm': gain((DEPTH, d)),
        'w_in': nrm((DEPTH, d, D_IN), d ** -0.5),
        'conv_w': nrm((DEPTH, CONV_W, CONV_CH), CONV_W ** -0.5),
        'conv_b': nrm((DEPTH, CONV_CH), 0.02),
        'm_igate_b': nrm((DEPTH, M_HEADS), 0.1),
        'm_fgate_b': jnp.linspace(3.0, 6.0, M_HEADS, dtype=f32)[None, :] + nrm((DEPTH, M_HEADS), 0.1),
        'm_norm': gain((DEPTH, M_WIDTH)),
        'r_wa': nrm((DEPTH, R_BLOCKS, R_BDIM, R_BDIM), R_BDIM ** -0.5),
        'r_ba': nrm((DEPTH, R_WIDTH), 0.1),
        'r_wx': nrm((DEPTH, R_BLOCKS, R_BDIM, R_BDIM), R_BDIM ** -0.5),
        'r_bx': nrm((DEPTH, R_WIDTH), 0.1),
        'r_lambda': jnp.log(s_lam) - jnp.log1p(-s_lam),
        'g_a_log': jnp.log(jax.random.uniform(next(ks), (DEPTH, G_HEADS), f32, 1.0, 16.0)),
        'g_dt_bias': dt + jnp.log(-jnp.expm1(-dt)),
        'g_norm': gain((DEPTH, G_DH)),
        'w_gate': nrm((DEPTH, d, N_BRANCH * d), d ** -0.5),
        'b_gate': nrm((DEPTH, N_BRANCH * d), 0.1),
        'w_branch': nrm((DEPTH, N_BRANCH, BR_WIDTH, d), BR_WIDTH ** -0.5),
        'w_out': nrm((DEPTH, d, d), d ** -0.5),
        'xa_norm': gain((DEPTH, d)),
        'xa_wq': nrm((DEPTH, d, d), d ** -0.5),
        'xa_wk': nrm((DEPTH, d, d), d ** -0.5),
        'xa_wv': nrm((DEPTH, d, d), d ** -0.5),
        'xa_wo': nrm((DEPTH, d, d), d ** -0.5),
        'ffn2_norm': gain((DEPTH, d)),
        'ffn2_w_gu': nrm((DEPTH, d, 2 * D_FF), d ** -0.5),
        'ffn2_w_down': nrm((DEPTH, D_FF, d), D_FF ** -0.5),
        'final_norm': gain((d,)),
    }


def reference(x_prompt, x_sample, cache_mem_k, cache_mem_v, state_conv, state_mlstm_C, state_mlstm_n,
              state_mlstm_m, state_rglru_h, state_delta_S, mem_prompt,
              ffn1_norm, ffn1_w_gu, ffn1_w_down, mix_norm, w_in, conv_w, conv_b, m_igate_b, m_fgate_b,
              m_norm, r_wa, r_ba, r_wx, r_bx, r_lambda, g_a_log, g_dt_bias, g_norm, w_gate, b_gate,
              w_branch, w_out, xa_norm, xa_wq, xa_wk, xa_wv, xa_wo, ffn2_norm, ffn2_w_gu, ffn2_w_down,
              final_norm):
    f32 = jnp.float32
    stacked = (ffn1_norm, ffn1_w_gu, ffn1_w_down, mix_norm, w_in, conv_w, conv_b, m_igate_b, m_fgate_b,
               m_norm, r_wa, r_ba, r_wx, r_bx, r_lambda, g_a_log, g_dt_bias, g_norm, w_gate, b_gate,
               w_branch, w_out, xa_norm, xa_wq, xa_wo, ffn2_norm, ffn2_w_gu, ffn2_w_down)

    Bp = x_prompt.shape[0]
    empty = (jnp.zeros((Bp, CONV_W - 1, CONV_CH), x_prompt.dtype),
             jnp.zeros((Bp, M_HEADS, M_DH, M_DH), f32),
             jnp.zeros((Bp, M_HEADS, M_DH), f32),
             jnp.full((Bp, M_HEADS), M_INIT, f32),
             jnp.zeros((Bp, R_WIDTH), f32),
             jnp.zeros((Bp, G_HEADS, G_DH, G_DH), f32))
    yp = x_prompt
    p_mk, p_mv, p_st = [], [], []
    for l in range(DEPTH):
        w = tuple(a[l] for a in stacked)
        mk = (mem_prompt @ xa_wk[l]).reshape(Bp, N_MEM, X_HEADS, X_DH)
        mv = (mem_prompt @ xa_wv[l]).reshape(Bp, N_MEM, X_HEADS, X_DH)
        yp, st = decoder_layer(yp, mk, mv, empty, w)
        p_mk.append(mk)
        p_mv.append(mv)
        p_st.append(st)
    y_prompt = rmsnorm(yp, final_norm)

    ys = x_sample
    s_st = []
    for l in range(DEPTH):
        w = tuple(a[l] for a in stacked)
        st_in = (state_conv[l], state_mlstm_C[l], state_mlstm_n[l], state_mlstm_m[l],
                 state_rglru_h[l], state_delta_S[l])
        ys, st = decoder_layer(ys, cache_mem_k[l], cache_mem_v[l], st_in, w)
        s_st.append(st)
    y_sample = rmsnorm(ys, final_norm)

    pd, sd = x_prompt.dtype, x_sample.dtype
    p_mem_k = jnp.stack(p_mk).astype(pd)
    p_mem_v = jnp.stack(p_mv).astype(pd)
    p_conv, p_C, p_n, p_m, p_h, p_S = [jnp.stack([st[i] for st in p_st]).astype(pd) for i in range(6)]
    s_conv, s_C, s_n, s_m, s_h, s_S = [jnp.stack([st[i] for st in s_st]).astype(sd) for i in range(6)]
    return (y_prompt, y_sample, p_mem_k, p_mem_v, p_conv, p_C, p_n, p_m, p_h, p_S,
            s_conv, s_C, s_n, s_m, s_h, s_S)
```

```python
import functools
import math

import jax
import jax.numpy as jnp
from jax import lax
from jax.experimental import pallas as pl
from jax.experimental.pallas import tpu as pltpu

F32 = jnp.float32
BF16 = jnp.bfloat16

D_MODEL = 1024
DEPTH = 4
N_MEM = 256
X_HEADS = 4
X_DH = D_MODEL // X_HEADS
M_HEADS = 4
M_DH = D_MODEL // M_HEADS
R_WIDTH = D_MODEL
R_BLOCKS = 8
R_BDIM = R_WIDTH // R_BLOCKS
R_C = 8.0
G_HEADS = 8
G_DH = D_MODEL // G_HEADS
N_BRANCH = 3
CONV_W = 4
CHUNK = 64
D_FF = 2816
EPS = 1e-6
M_INIT = -1e30

CONV_CH = R_WIDTH + 3 * D_MODEL
OFF_GQ = R_WIDTH
OFF_GK = OFF_GQ + D_MODEL
OFF_GV = OFF_GK + D_MODEL
OFF_MQ = OFF_GV + D_MODEL
OFF_MK = OFF_MQ + D_MODEL
OFF_MV = OFF_MK + D_MODEL
OFF_MO = OFF_MV + D_MODEL
OFF_RY = OFF_MO + D_MODEL
OFF_GZ = OFF_RY + D_MODEL
OFF_MI = OFF_GZ + D_MODEL
N_SMALL = 2 * M_HEADS + 2 * G_HEADS
SMALL_PAD = 32
SM_MI, SM_MF, SM_GB, SM_GA = 0, M_HEADS, 2 * M_HEADS, 2 * M_HEADS + G_HEADS

LANE = 128
SUBLANE = 8
VMEM_LIMIT = 52 * 1024 * 1024


def _cparams(n_axes, vmem=VMEM_LIMIT):
    return pltpu.CompilerParams(dimension_semantics=("arbitrary",) * n_axes,
                                vmem_limit_bytes=vmem)


def _resident(shape, index_map):
    return pl.BlockSpec(shape, index_map, pipeline_mode=pl.Buffered(1))


def _rms(x, g):
    return x * lax.rsqrt(jnp.mean(x * x, axis=-1, keepdims=True) + EPS) * g


def _softplus(x):
    return jnp.maximum(x, 0.0) + jnp.log1p(jnp.exp(-jnp.abs(x)))


def _dot(a, b):
    return jnp.dot(a, b, preferred_element_type=F32)


def _dot_nt(a, b):
    return lax.dot_general(a, b, (((1,), (1,)), ((), ())), preferred_element_type=F32)


def _dot_tn(a, b):
    return lax.dot_general(a, b, (((0,), (0,)), ((), ())), preferred_element_type=F32)


def _mm_body(x_ref, w_ref, o_ref):
    o_ref[...] = _dot(x_ref[...], w_ref[...]).astype(o_ref.dtype)


def matmul(x, w, out_dtype, tm, tn):
    M, K = x.shape
    N = w.shape[1]
    tm = min(tm, M)
    return pl.pallas_call(
        _mm_body,
        grid=(N // tn, M // tm),
        in_specs=[pl.BlockSpec((tm, K), lambda j, i: (i, 0)),
                  pl.BlockSpec((K, tn), lambda j, i: (0, j))],
        out_specs=pl.BlockSpec((tm, tn), lambda j, i: (i, j)),
        out_shape=jax.ShapeDtypeStruct((M, N), out_dtype),
        compiler_params=_cparams(2),
        name="matmul",
    )(x, w)


def _mm_t_body(w_ref, x_ref, o_ref):
    o_ref[...] = _dot_nt(w_ref[...], x_ref[...])


def matmul_t(wt, x, tm):
    P, K = wt.shape
    M = x.shape[0]
    tm = min(tm, M)
    return pl.pallas_call(
        _mm_t_body,
        grid=(M // tm,),
        in_specs=[pl.BlockSpec((P, K), lambda i: (0, 0)),
                  pl.BlockSpec((tm, K), lambda i: (i, 0))],
        out_specs=pl.BlockSpec((P, tm), lambda i: (0, i)),
        out_shape=jax.ShapeDtypeStruct((P, M), F32),
        compiler_params=_cparams(1),
        name="matmul_t",
    )(wt, x)


def _ffn_body(*refs, has_proj, norm_dtype):
    it = iter(refs)
    x_ref = next(it)
    if has_proj:
        o_ref, wo_ref = next(it), next(it)
    g_ref, wgu_ref, wd_ref = next(it), next(it), next(it)
    if norm_dtype is not None:
        g2_ref = next(it)
    xo_ref = next(it)
    if norm_dtype is not None:
        no_ref = next(it)
    x = x_ref[...]
    if has_proj:
        x = x + _dot(o_ref[...], wo_ref[...])
    xn = _rms(x, g_ref[...]).astype(BF16)
    gu = _dot(xn, wgu_ref[...])
    g = gu[:, :D_FF]
    u = gu[:, D_FF:]
    a = (g * jax.nn.sigmoid(g) * u).astype(BF16)
    y = x + 0.5 * _dot(a, wd_ref[...])
    xo_ref[...] = y
    if norm_dtype is not None:
        no_ref[...] = _rms(y, g2_ref[...]).astype(norm_dtype)


def ffn(x, norm_g, w_gu, w_d, *, proj=None, next_g=None, norm_dtype=None, tm=512):
    M, D = x.shape
    tm = min(tm, M)
    row = lambda i: (i, 0)
    fixed = lambda i: (0, 0)
    args = [x]
    in_specs = [pl.BlockSpec((tm, D), row)]
    if proj is not None:
        o, wo = proj
        args += [o, wo]
        in_specs += [pl.BlockSpec((tm, D), row), _resident((D, D), fixed)]
    args += [norm_g, w_gu, w_d]
    in_specs += [_resident((1, D), fixed), _resident(w_gu.shape, fixed), _resident(w_d.shape, fixed)]
    out_shape = [jax.ShapeDtypeStruct((M, D), F32)]
    out_specs = [pl.BlockSpec((tm, D), row)]
    if norm_dtype is not None:
        args.append(next_g)
        in_specs.append(_resident((1, D), fixed))
        out_shape.append(jax.ShapeDtypeStruct((M, D), norm_dtype))
        out_specs.append(pl.BlockSpec((tm, D), row))
    res = pl.pallas_call(
        functools.partial(_ffn_body, has_proj=proj is not None, norm_dtype=norm_dtype),
        grid=(M // tm,),
        in_specs=in_specs, out_specs=out_specs, out_shape=out_shape,
        compiler_params=_cparams(1),
        name="ffn",
    )(*args)
    return res if norm_dtype is not None else (res[0], None)


def _conv_body(u_ref, c0_ref, w_ref, b_ref, o_ref, s_ref, *, L):
    pad = SUBLANE
    lo = pad - (CONV_W - 1)
    s_ref[:, lo:pad, :] = c0_ref[...]
    s_ref[:, pad:pad + L, :] = u_ref[...]
    acc = b_ref[...][None] + u_ref[...] * w_ref[CONV_W - 1:CONV_W, :][None]
    for j in range(CONV_W - 1):
        acc = acc + s_ref[:, lo + j:lo + j + L, :] * w_ref[j:j + 1, :][None]
    o_ref[...] = acc


def causal_conv(proj3, conv0, w, b, *, bt, tc):
    B, L, _ = proj3.shape
    return pl.pallas_call(
        functools.partial(_conv_body, L=L),
        grid=(B // bt, CONV_CH // tc),
        in_specs=[pl.BlockSpec((bt, L, tc), lambda i, j: (i, 0, j)),
                  pl.BlockSpec((bt, CONV_W - 1, tc), lambda i, j: (i, 0, j)),
                  pl.BlockSpec((CONV_W, tc), lambda i, j: (0, j)),
                  pl.BlockSpec((1, tc), lambda i, j: (0, j))],
        out_specs=pl.BlockSpec((bt, L, tc), lambda i, j: (i, 0, j)),
        out_shape=jax.ShapeDtypeStruct((B, L, CONV_CH), F32),
        scratch_shapes=[pltpu.VMEM((bt, L + SUBLANE, tc), F32)],
        compiler_params=_cparams(2),
        name="causal_conv",
    )(proj3, conv0, w, b)


def _rglru_body(rx_ref, ry_ref, wa_ref, wx_ref, ba_ref, bx_ref, lam_ref, h0_ref,
                hr_ref, h1_ref, a_s, b_s, h_s, *, bt, L):
    R = bt * L
    rx = rx_ref[...].reshape(R, R_BDIM)
    rxb = rx.astype(BF16)
    r = jax.nn.sigmoid(_dot(rxb, wa_ref[0]) + ba_ref[...])
    ig = jax.nn.sigmoid(_dot(rxb, wx_ref[0]) + bx_ref[...])
    log_a = -R_C * r * _softplus(-lam_ref[...])
    a = jnp.exp(log_a)
    a_s[...] = a
    b_s[...] = jnp.sqrt(jnp.tanh(-log_a) * (a * a + 1.0)) * (ig * rx)
    gps = L // SUBLANE
    row = lax.broadcasted_iota(jnp.int32, (SUBLANE, R_BDIM), 0)

    def seq_body(bi, carry):
        def grp(g, h):
            r0 = pl.multiple_of((bi * gps + g) * SUBLANE, SUBLANE)
            A = a_s[pl.ds(r0, SUBLANE), :]
            Bv = b_s[pl.ds(r0, SUBLANE), :]
            for s in (1, 2, 4):
                keep = row >= s
                Bv = jnp.where(keep, A * pltpu.roll(Bv, s, 0) + Bv, Bv)
                A = jnp.where(keep, A * pltpu.roll(A, s, 0), A)
            hh = A * h + Bv
            h_s[pl.ds(r0, SUBLANE), :] = hh
            return hh[SUBLANE - 1:SUBLANE, :]

        h = lax.fori_loop(0, gps, grp, h0_ref[bi])
        h1_ref[bi] = h
        return carry

    lax.fori_loop(0, bt, seq_body, 0)
    hr = h_s[...] * jax.nn.gelu(ry_ref[...].reshape(R, R_BDIM))
    hr_ref[...] = hr.reshape(bt, L, R_BDIM).astype(hr_ref.dtype)


def rglru(conv3, proj3, wa, wx, ba, bx, lam, h0, *, bt):
    B, L, _ = conv3.shape
    R = bt * L
    ry_blk = OFF_RY // R_BDIM
    vec = lambda i, n: (0, n)
    return pl.pallas_call(
        functools.partial(_rglru_body, bt=bt, L=L),
        grid=(B // bt, R_BLOCKS),
        in_specs=[pl.BlockSpec((bt, L, R_BDIM), lambda i, n: (i, 0, n)),
                  pl.BlockSpec((bt, L, R_BDIM), lambda i, n: (i, 0, ry_blk + n)),
                  pl.BlockSpec((1, R_BDIM, R_BDIM), lambda i, n: (n, 0, 0)),
                  pl.BlockSpec((1, R_BDIM, R_BDIM), lambda i, n: (n, 0, 0)),
                  pl.BlockSpec((1, R_BDIM), vec),
                  pl.BlockSpec((1, R_BDIM), vec),
                  pl.BlockSpec((1, R_BDIM), vec),
                  pl.BlockSpec((bt, 1, R_BDIM), lambda i, n: (i, 0, n))],
        out_specs=[pl.BlockSpec((bt, L, R_BDIM), lambda i, n: (i, 0, n)),
                   pl.BlockSpec((bt, 1, R_BDIM), lambda i, n: (i, 0, n))],
        out_shape=[jax.ShapeDtypeStruct((B, L, R_WIDTH), F32),
                   jax.ShapeDtypeStruct((B, 1, R_WIDTH), F32)],
        scratch_shapes=[pltpu.VMEM((R, R_BDIM), F32)] * 3,
        compiler_params=_cparams(2),
        name="rglru",
    )(conv3, proj3, wa, wx, ba, bx, lam, h0)


def _chunk_masks(c):
    t = lax.broadcasted_iota(jnp.int32, (c, c), 0)
    s = lax.broadcasted_iota(jnp.int32, (c, c), 1)
    return s <= t, s == t, s < t


def _col_from_row(row, eye):
    return jnp.sum(jnp.where(eye, row, 0.0), axis=1, keepdims=True)


def _row_from_col(col, eye):
    return jnp.sum(jnp.where(eye, col, 0.0), axis=0, keepdims=True)


def _mlstm_body(q_ref, k_ref, v_ref, mo_ref, li_ref, lf_ref, c0_ref, n0_ref, m0_ref,
                ib_ref, fb_ref, g_ref, hm_ref, c1_ref, n1_ref, m1_ref, *, bt, L, c):
    hd = pl.program_id(1)
    ib = ib_ref[hd]
    fb = fb_ref[hd]
    nc = L // c
    tri, eye, _ = _chunk_masks(c)
    scale = M_DH ** -0.5
    gain = g_ref[...]

    def seq_body(bi, carry):
        c1_ref[bi, 0] = c0_ref[bi, 0]
        n1_ref[bi, 0] = n0_ref[bi, 0]
        m1_ref[bi, 0] = m0_ref[bi, 0]

        def chunk(ci, carry2):
            r0 = pl.multiple_of((bi * nc + ci) * c, c)
            gi = bi * nc + ci
            qc = q_ref[pl.ds(r0, c), :].astype(BF16)
            kf = k_ref[pl.ds(r0, c), :] * scale
            kc = kf.astype(BF16)
            vc = v_ref[pl.ds(r0, c), :].astype(BF16)
            C = c1_ref[bi, 0]
            n = n1_ref[bi, 0]
            m = m1_ref[bi, 0][:, 0:1]
            li_row = li_ref[0, pl.ds(gi, 1), :] + ib
            lf_row = -_softplus(-(lf_ref[0, pl.ds(gi, 1), :] + fb))
            b_col = jnp.sum(jnp.where(tri, lf_row, 0.0), axis=1, keepdims=True)
            b_row = _row_from_col(b_col, eye)
            li_col = _col_from_row(li_row, eye)
            logD = jnp.where(tri, b_col - b_row + li_row, -jnp.inf)
            inter = b_col + m
            mt = jnp.maximum(inter, jnp.max(logD, axis=1, keepdims=True))
            wi = jnp.exp(inter - mt)
            s = _dot_nt(qc, kc) * jnp.exp(logD - mt)
            num = _dot(s.astype(BF16), vc) + wi * _dot(qc, C.astype(BF16))
            qn = jnp.sum(qc.astype(F32) * n, axis=1, keepdims=True)
            den = jnp.sum(s, axis=1, keepdims=True) + wi * qn
            hh = num / jnp.maximum(jnp.abs(den), jnp.exp(-mt))
            y = _rms(hh, gain) * jax.nn.sigmoid(mo_ref[pl.ds(r0, c), :])
            hm_ref[pl.ds(r0, c), :] = y.astype(hm_ref.dtype)
            m_new = mt[c - 1:c, :]
            b_last = b_col[c - 1:c, :]
            w_s = jnp.exp(b_last - b_col + li_col - m_new)
            dec = jnp.exp(b_last + m - m_new)
            kw = kf * w_s
            c1_ref[bi, 0] = dec * C + _dot_tn(kw.astype(BF16), vc)
            n1_ref[bi, 0] = dec * n + jnp.sum(kw, axis=0, keepdims=True)
            m1_ref[bi, 0] = jnp.broadcast_to(m_new, (1, LANE))
            return carry2

        lax.fori_loop(0, nc, chunk, 0)
        return carry

    lax.fori_loop(0, bt, seq_body, 0)


def mlstm(proj, small_c, C0, n0, m0, ib, fb, gain, *, B, L, c, bt):
    R = bt * L
    nck = R // c
    qb, kb, vb, ob = (OFF_MQ // M_DH, OFF_MK // M_DH, OFF_MV // M_DH, OFF_MO // M_DH)
    smem = pl.BlockSpec(memory_space=pltpu.SMEM)
    st4 = lambda i, h: (i, h, 0, 0)
    return pl.pallas_call(
        functools.partial(_mlstm_body, bt=bt, L=L, c=c),
        grid=(B // bt, M_HEADS),
        in_specs=[pl.BlockSpec((R, M_DH), lambda i, h: (i, qb + h)),
                  pl.BlockSpec((R, M_DH), lambda i, h: (i, kb + h)),
                  pl.BlockSpec((R, M_DH), lambda i, h: (i, vb + h)),
                  pl.BlockSpec((R, M_DH), lambda i, h: (i, ob + h)),
                  pl.BlockSpec((1, nck, c), lambda i, h: (SM_MI + h, i, 0)),
                  pl.BlockSpec((1, nck, c), lambda i, h: (SM_MF + h, i, 0)),
                  pl.BlockSpec((bt, 1, M_DH, M_DH), st4),
                  pl.BlockSpec((bt, 1, 1, M_DH), st4),
                  pl.BlockSpec((bt, 1, 1, LANE), st4),
                  smem, smem,
                  pl.BlockSpec((1, M_DH), lambda i, h: (0, h))],
        out_specs=[pl.BlockSpec((R, M_DH), lambda i, h: (i, h)),
                   pl.BlockSpec((bt, 1, M_DH, M_DH), st4),
                   pl.BlockSpec((bt, 1, 1, M_DH), st4),
                   pl.BlockSpec((bt, 1, 1, LANE), st4)],
        out_shape=[jax.ShapeDtypeStruct((B * L, D_MODEL), F32),
                   jax.ShapeDtypeStruct((B, M_HEADS, M_DH, M_DH), F32),
                   jax.ShapeDtypeStruct((B, M_HEADS, 1, M_DH), F32),
                   jax.ShapeDtypeStruct((B, M_HEADS, 1, LANE), F32)],
        compiler_params=_cparams(2),
        name="mlstm",
    )(proj, proj, proj, proj, small_c, small_c, C0, n0, m0, ib, fb, gain)


def _l2norm(x):
    return x * lax.rsqrt(jnp.sum(x * x, axis=-1, keepdims=True) + EPS)


def _silu(x):
    return x * jax.nn.sigmoid(x)


def _gdn_prep_body(q_ref, k_ref, v_ref, gb_ref, ga_ref, alog_ref, dtb_ref,
                   lm_ref, qk_ref, vb_ref, kbg_ref, qe_ref, kd_ref, eg_ref, *, nck, c):
    hd = pl.program_id(1)
    neg_a = -jnp.exp(alog_ref[hd])
    dtb = dtb_ref[hd]
    tri, eye, strict = _chunk_masks(c)

    def chunk(gi, carry):
        r0 = pl.multiple_of(gi * c, c)
        q = _l2norm(_silu(q_ref[pl.ds(r0, c), :])) * (G_DH ** -0.5)
        k = _l2norm(_silu(k_ref[pl.ds(r0, c), :]))
        v = _silu(v_ref[pl.ds(r0, c), :])
        beta_row = jax.nn.sigmoid(gb_ref[0, pl.ds(gi, 1), :])
        g_row = neg_a * _softplus(ga_ref[0, pl.ds(gi, 1), :] + dtb)
        G_col = jnp.sum(jnp.where(tri, g_row, 0.0), axis=1, keepdims=True)
        G_row = _row_from_col(G_col, eye)
        beta_col = _col_from_row(beta_row, eye)
        dec = jnp.exp(jnp.where(tri, G_col - G_row, -jnp.inf))
        kb16 = k.astype(BF16)
        kk = _dot_nt(kb16, kb16)
        lm_ref[0, pl.ds(r0, c), :] = jnp.where(strict, beta_col * kk * dec, 0.0)
        qk_ref[0, pl.ds(r0, c), :] = (_dot_nt(q.astype(BF16), kb16) * dec).astype(qk_ref.dtype)
        eG = jnp.exp(G_col)
        vb_ref[pl.ds(r0, c), :] = (v * beta_col).astype(vb_ref.dtype)
        kbg_ref[pl.ds(r0, c), :] = (k * beta_col * eG).astype(kbg_ref.dtype)
        qe_ref[pl.ds(r0, c), :] = (q * eG).astype(qe_ref.dtype)
        gl = G_col[c - 1:c, :]
        kd_ref[pl.ds(r0, c), :] = (k * jnp.exp(gl - G_col)).astype(kd_ref.dtype)
        eg_ref[0, pl.ds(gi, 1), :] = jnp.exp(G_row)
        return carry

    lax.fori_loop(0, nck, chunk, 0)


def gdn_prep(conv2, small_c, alog, dtb, *, B, L, c, bt, idt):
    M = B * L
    R = bt * L
    nck = R // c
    smem = pl.BlockSpec(memory_space=pltpu.SMEM)
    col = lambda off: (lambda i, h: (i, off // G_DH + h))
    hrow = lambda i, h: (h, i, 0)
    tok = pl.BlockSpec((R, G_DH), lambda i, h: (i, h))
    return pl.pallas_call(
        functools.partial(_gdn_prep_body, nck=nck, c=c),
        grid=(B // bt, G_HEADS),
        in_specs=[pl.BlockSpec((R, G_DH), col(OFF_GQ)),
                  pl.BlockSpec((R, G_DH), col(OFF_GK)),
                  pl.BlockSpec((R, G_DH), col(OFF_GV)),
                  pl.BlockSpec((1, nck, c), lambda i, h: (SM_GB + h, i, 0)),
                  pl.BlockSpec((1, nck, c), lambda i, h: (SM_GA + h, i, 0)),
                  smem, smem],
        out_specs=[pl.BlockSpec((1, R, c), hrow),
                   pl.BlockSpec((1, R, c), hrow),
                   tok, tok, tok, tok,
                   pl.BlockSpec((1, nck, c), hrow)],
        out_shape=[jax.ShapeDtypeStruct((G_HEADS, M, c), F32),
                   jax.ShapeDtypeStruct((G_HEADS, M, c), idt),
                   jax.ShapeDtypeStruct((M, D_MODEL), idt),
                   jax.ShapeDtypeStruct((M, D_MODEL), idt),
                   jax.ShapeDtypeStruct((M, D_MODEL), idt),
                   jax.ShapeDtypeStruct((M, D_MODEL), idt),
                   jax.ShapeDtypeStruct((G_HEADS, M // c, c), F32)],
        compiler_params=_cparams(2),
        name="gdn_prep",
    )(conv2, conv2, conv2, small_c, small_c, alog, dtb)


def _fsub_body(l_ref, x_ref, *, c):
    jrow = lax.broadcasted_iota(jnp.int32, (c, LANE), 0)
    x_ref[...] = jnp.zeros(x_ref.shape, F32)

    def t_body(t, carry):
        def sb_body(sb, acc):
            s0 = pl.multiple_of(sb * SUBLANE, SUBLANE)
            lblk = l_ref[t, pl.ds(s0, SUBLANE), :]
            for kk in range(SUBLANE):
                acc = acc - lblk[kk:kk + 1, :] * x_ref[s0 + kk]
            return acc

        acc0 = jnp.where(jrow == t, 1.0, 0.0)
        x_ref[t] = lax.fori_loop(0, (t + SUBLANE - 1) // SUBLANE, sb_body, acc0)
        return carry

    lax.fori_loop(0, c, t_body, 0)


def tri_inverse(lt):
    c, _, n = lt.shape
    blk = pl.BlockSpec((c, c, LANE), lambda i: (0, 0, i))
    return pl.pallas_call(
        functools.partial(_fsub_body, c=c),
        grid=(n // LANE,),
        in_specs=[blk], out_specs=blk,
        out_shape=jax.ShapeDtypeStruct((c, c, n), F32),
        compiler_params=_cparams(1),
        name="tri_inverse",
    )(lt)


def _gdn_rec_body(t_ref, qk_ref, vb_ref, kbg_ref, qe_ref, kd_ref, eg_ref, gz_ref, s0_ref,
                  g_ref, hd_ref, s1_ref, *, bt, L, c):
    nc = L // c
    gain = g_ref[...]

    def seq_body(bi, carry):
        s1_ref[bi, 0] = s0_ref[bi, 0]

        def chunk(ci, carry2):
            gi = bi * nc + ci
            r0 = pl.multiple_of(gi * c, c)
            rows = pl.ds(r0, c)
            T = t_ref[0, rows, :].astype(BF16)
            S = s1_ref[bi, 0]
            Sb = S.astype(BF16)
            u0 = _dot(T, vb_ref[rows, :].astype(BF16))
            wk = _dot(T, kbg_ref[rows, :].astype(BF16))
            u = (u0 - _dot(wk.astype(BF16), Sb)).astype(BF16)
            o = _dot(qe_ref[rows, :].astype(BF16), Sb) + _dot(qk_ref[0, rows, :].astype(BF16), u)
            egl = eg_ref[0, pl.ds(gi, 1), :][:, c - 1:c]
            s1_ref[bi, 0] = egl * S + _dot_tn(kd_ref[rows, :].astype(BF16), u)
            y = _rms(o, gain) * _silu(gz_ref[rows, :])
            hd_ref[rows, :] = y.astype(hd_ref.dtype)
            return carry2

        lax.fori_loop(0, nc, chunk, 0)
        return carry

    lax.fori_loop(0, bt, seq_body, 0)


def gdn_rec(T, qk, vb, kbg, qe, kd, eg, proj, S0, gain, *, B, L, c, bt):
    M = B * L
    R = bt * L
    nck = R // c
    hrow = lambda i, h: (h, i, 0)
    tok = pl.BlockSpec((R, G_DH), lambda i, h: (i, h))
    st4 = lambda i, h: (i, h, 0, 0)
    gz_blk = OFF_GZ // G_DH
    return pl.pallas_call(
        functools.partial(_gdn_rec_body, bt=bt, L=L, c=c),
        grid=(B // bt, G_HEADS),
        in_specs=[pl.BlockSpec((1, R, c), hrow),
                  pl.BlockSpec((1, R, c), hrow),
                  tok, tok, tok, tok,
                  pl.BlockSpec((1, nck, c), hrow),
                  pl.BlockSpec((R, G_DH), lambda i, h: (i, gz_blk + h)),
                  pl.BlockSpec((bt, 1, G_DH, G_DH), st4),
                  pl.BlockSpec((1, G_DH), lambda i, h: (0, 0))],
        out_specs=[tok, pl.BlockSpec((bt, 1, G_DH, G_DH), st4)],
        out_shape=[jax.ShapeDtypeStruct((M, D_MODEL), F32),
                   jax.ShapeDtypeStruct((B, G_HEADS, G_DH, G_DH), F32)],
        compiler_params=_cparams(2),
        name="gdn_rec",
    )(T, qk, vb, kbg, qe, kd, eg, proj, S0, gain)


def _merge_body(x_ref, xn_ref, hm_ref, hr_ref, hd_ref, wg_ref, bg_ref, wb_ref, wo_ref,
                gx_ref, wq_ref, xo_ref, q_ref):
    xn = xn_ref[...]
    merged = None
    for n, br in enumerate((hm_ref, hr_ref, hd_ref)):
        lo = n * D_MODEL
        gate = jax.nn.sigmoid(_dot(xn, wg_ref[:, lo:lo + D_MODEL]) + bg_ref[:, lo:lo + D_MODEL])
        term = gate * _dot(br[...].astype(BF16), wb_ref[n])
        merged = term if merged is None else merged + term
    x = x_ref[...] + _dot(merged.astype(BF16), wo_ref[...])
    xo_ref[...] = x
    q_ref[...] = _dot(_rms(x, gx_ref[...]).astype(BF16), wq_ref[...]).astype(q_ref.dtype)


def merge(x, xn, hm, hr, hd, w_gate, b_gate, w_branch, w_out, xa_g, xa_wq, *, tm=512):
    M, D = x.shape
    tm = min(tm, M)
    row = pl.BlockSpec((tm, D), lambda i: (i, 0))
    f2 = lambda i: (0, 0)
    return pl.pallas_call(
        _merge_body,
        grid=(M // tm,),
        in_specs=[row, row, row, row, row,
                  _resident(w_gate.shape, f2), _resident(b_gate.shape, f2),
                  _resident(w_branch.shape, lambda i: (0, 0, 0)), _resident(w_out.shape, f2),
                  _resident(xa_g.shape, f2), _resident(xa_wq.shape, f2)],
        out_specs=[row, row],
        out_shape=[jax.ShapeDtypeStruct((M, D), F32), jax.ShapeDtypeStruct((M, D), F32)],
        compiler_params=_cparams(1),
        name="merge",
    )(x, xn, hm, hr, hd, w_gate, b_gate, w_branch, w_out, xa_g, xa_wq)


def _xattn_body(q_ref, k_ref, v_ref, o_ref, *, bt, tq):
    scale = X_DH ** -0.5
    rows = []
    for bi in range(bt):
        q = q_ref[bi * tq:(bi + 1) * tq, :]
        heads = []
        for h in range(X_HEADS):
            lo = h * X_DH
            qh = q[:, lo:lo + X_DH].astype(BF16)
            kh = k_ref[bi, :, lo:lo + X_DH].astype(BF16)
            vh = v_ref[bi, :, lo:lo + X_DH].astype(BF16)
            s = _dot_nt(qh, kh) * scale
            e = jnp.exp(s - jnp.max(s, axis=-1, keepdims=True))
            p = e / jnp.sum(e, axis=-1, keepdims=True)
            heads.append(_dot(p.astype(BF16), vh))
        rows.append(jnp.concatenate(heads, axis=1))
    o = rows[0] if bt == 1 else jnp.concatenate(rows, axis=0)
    o_ref[...] = o.astype(o_ref.dtype)


def xattn(q, mem_k, mem_v, *, B, L, bt, tq):
    nq = L // tq
    R = bt * tq
    kv = pl.BlockSpec((bt, N_MEM, D_MODEL), lambda i, j: (i, 0, 0))
    qo = pl.BlockSpec((R, D_MODEL), lambda i, j: (i * nq + j, 0))
    return pl.pallas_call(
        functools.partial(_xattn_body, bt=bt, tq=tq),
        grid=(B // bt, nq),
        in_specs=[qo, kv, kv],
        out_specs=qo,
        out_shape=jax.ShapeDtypeStruct((B * L, D_MODEL), BF16),
        compiler_params=_cparams(2),
        name="xattn",
    )(q, mem_k, mem_v)


def _group_cfg(B, L):
    if L >= CHUNK:
        return dict(c=math.gcd(CHUNK, L), seq_bt=1, conv_bt=1, conv_tc=512, xa_bt=1, xa_tq=min(512, L),
                    idt=BF16)
    return dict(c=math.gcd(CHUNK, L), seq_bt=16, conv_bt=B, conv_tc=512, xa_bt=4, xa_tq=L,
                idt=F32)


def _layer(x, mem_k, mem_v, state, w, *, B, L, last_g):
    cfg = _group_cfg(B, L)
    c, sbt, idt = cfg["c"], cfg["seq_bt"], cfg["idt"]
    M = B * L
    conv0, C0, n0, m0, h0, S0 = state

    x1, xn = ffn(x, w["f1_n"], w["f1_gu"], w["f1_d"], next_g=w["mix_n"], norm_dtype=BF16)

    proj = matmul(xn, w["w_in_main"], F32, tm=512, tn=1024)
    small = matmul_t(w["w_in_small_t"], xn, tm=512)
    small_c = small.reshape(SMALL_PAD, M // c, c)
    proj3 = proj.reshape(B, L, -1)

    conv3 = causal_conv(proj3, conv0, w["conv_w"], w["conv_b"], bt=cfg["conv_bt"], tc=cfg["conv_tc"])
    conv_new = proj3[:, L - (CONV_W - 1):, :CONV_CH]
    conv2 = conv3.reshape(M, CONV_CH)

    hr3, h1 = rglru(conv3, proj3, w["r_wa"], w["r_wx"], w["r_ba"], w["r_bx"], w["r_lam"],
                    h0.reshape(B, 1, R_WIDTH), bt=cfg["conv_bt"])
    hr = hr3.reshape(M, R_WIDTH)

    m0b = jnp.broadcast_to(m0[:, :, None, None], (B, M_HEADS, 1, LANE))
    hm, C1, n1, m1b = mlstm(proj, small_c, C0, n0.reshape(B, M_HEADS, 1, M_DH), m0b,
                            w["m_ib"], w["m_fb"], w["m_norm"], B=B, L=L, c=c, bt=sbt)

    lm, qk, vb, kbg, qe, kd, eg = gdn_prep(conv2, small_c, w["g_alog"], w["g_dtb"],
                                           B=B, L=L, c=c, bt=sbt, idt=idt)
    ninst = G_HEADS * (M // c)
    lt = lm.reshape(G_HEADS, M // c, c, c).transpose(2, 3, 0, 1).reshape(c, c, ninst)
    tt = tri_inverse(lt)
    T = tt.reshape(c, c, G_HEADS, M // c).transpose(2, 3, 0, 1).reshape(G_HEADS, M, c).astype(idt)
    hd, S1 = gdn_rec(T, qk, vb, kbg, qe, kd, eg, proj, S0, w["g_norm"], B=B, L=L, c=c, bt=sbt)

    x2, q = merge(x1, xn, hm, hr, hd, w["w_gate"], w["b_gate"], w["w_branch"], w["w_out"],
                  w["xa_n"], w["xa_wq"])
    o = xattn(q, mem_k, mem_v, B=B, L=L, bt=cfg["xa_bt"], tq=cfg["xa_tq"])
    if last_g is None:
        x3, y = ffn(x2, w["f2_n"], w["f2_gu"], w["f2_d"], proj=(o, w["xa_wo"]))
    else:
        x3, y = ffn(x2, w["f2_n"], w["f2_gu"], w["f2_d"], proj=(o, w["xa_wo"]),
                    next_g=last_g, norm_dtype=F32)
    new_state = (conv_new, C1, n1.reshape(B, M_HEADS, M_DH), m1b[:, :, 0, 0],
                 h1.reshape(B, R_WIDTH), S1)
    return x3, y, new_state


def _prep_weights(l, ffn1_norm, ffn1_w_gu, ffn1_w_down, mix_norm, w_in, conv_w, conv_b, m_igate_b,
                  m_fgate_b, m_norm, r_wa, r_ba, r_wx, r_bx, r_lambda, g_a_log, g_dt_bias, g_norm,
                  w_gate, b_gate, w_branch, w_out, xa_norm, xa_wq, xa_wo, ffn2_norm, ffn2_w_gu,
                  ffn2_w_down):
    row = lambda a: a[l].reshape(1, -1).astype(F32)
    b16 = lambda a: a[l].astype(BF16)
    small_t = jnp.zeros((SMALL_PAD, D_MODEL), BF16).at[:N_SMALL].set(
        w_in[l][:, OFF_MI:OFF_MI + N_SMALL].T.astype(BF16))
    return dict(
        f1_n=row(ffn1_norm), f1_gu=b16(ffn1_w_gu), f1_d=b16(ffn1_w_down), mix_n=row(mix_norm),
        w_in_main=w_in[l][:, :OFF_MI].astype(BF16), w_in_small_t=small_t,
        conv_w=conv_w[l].astype(F32), conv_b=row(conv_b),
        m_ib=m_igate_b[l].astype(F32), m_fb=m_fgate_b[l].astype(F32), m_norm=row(m_norm),
        r_wa=b16(r_wa), r_wx=b16(r_wx), r_ba=row(r_ba), r_bx=row(r_bx), r_lam=row(r_lambda),
        g_alog=g_a_log[l].astype(F32), g_dtb=g_dt_bias[l].astype(F32), g_norm=row(g_norm),
        w_gate=b16(w_gate), b_gate=row(b_gate), w_branch=b16(w_branch), w_out=b16(w_out),
        xa_n=row(xa_norm), xa_wq=b16(xa_wq), xa_wo=b16(xa_wo),
        f2_n=row(ffn2_norm), f2_gu=b16(ffn2_w_gu), f2_d=b16(ffn2_w_down))


def kernel(x_prompt, x_sample, cache_mem_k, cache_mem_v, state_conv, state_mlstm_C, state_mlstm_n, state_mlstm_m, state_rglru_h, state_delta_S, mem_prompt, ffn1_norm, ffn1_w_gu, ffn1_w_down, mix_norm, w_in, conv_w, conv_b, m_igate_b, m_fgate_b, m_norm, r_wa, r_ba, r_wx, r_bx, r_lambda, g_a_log, g_dt_bias, g_norm, w_gate, b_gate, w_branch, w_out, xa_norm, xa_wq, xa_wk, xa_wv, xa_wo, ffn2_norm, ffn2_w_gu, ffn2_w_down, final_norm):
    Bp, Lp, D = x_prompt.shape
    Bs, Ls, _ = x_sample.shape
    stacked = (ffn1_norm, ffn1_w_gu, ffn1_w_down, mix_norm, w_in, conv_w, conv_b, m_igate_b,
               m_fgate_b, m_norm, r_wa, r_ba, r_wx, r_bx, r_lambda, g_a_log, g_dt_bias, g_norm,
               w_gate, b_gate, w_branch, w_out, xa_norm, xa_wq, xa_wo, ffn2_norm, ffn2_w_gu,
               ffn2_w_down)
    fin = final_norm.reshape(1, D).astype(F32)
    memp = mem_prompt.reshape(Bp * N_MEM, D).astype(BF16)
    empty = (jnp.zeros((Bp, CONV_W - 1, CONV_CH), F32),
             jnp.zeros((Bp, M_HEADS, M_DH, M_DH), F32),
             jnp.zeros((Bp, M_HEADS, M_DH), F32),
             jnp.full((Bp, M_HEADS), M_INIT, F32),
             jnp.zeros((Bp, R_WIDTH), F32),
             jnp.zeros((Bp, G_HEADS, G_DH, G_DH), F32))

    yp = x_prompt.reshape(Bp * Lp, D)
    ys = x_sample.reshape(Bs * Ls, D)
    p_mk, p_mv, p_st, s_st = [], [], [], []
    outp = outs = None
    for l in range(DEPTH):
        w = _prep_weights(l, *stacked)
        last_g = fin if l == DEPTH - 1 else None
        wkv = jnp.concatenate([xa_wk[l], xa_wv[l]], axis=1).astype(BF16)
        mkv = matmul(memp, wkv, F32, tm=512, tn=1024)
        mk = mkv[:, :D].reshape(Bp, N_MEM, D)
        mv = mkv[:, D:].reshape(Bp, N_MEM, D)
        p_mk.append(mk.reshape(Bp, N_MEM, X_HEADS, X_DH))
        p_mv.append(mv.reshape(Bp, N_MEM, X_HEADS, X_DH))
        yp, outp, st = _layer(yp, mk, mv, empty, w, B=Bp, L=Lp, last_g=last_g)
        p_st.append(st)
        st_in = (state_conv[l], state_mlstm_C[l], state_mlstm_n[l], state_mlstm_m[l],
                 state_rglru_h[l], state_delta_S[l])
        ys, outs, st = _layer(ys, cache_mem_k[l].reshape(Bs, N_MEM, D),
                              cache_mem_v[l].reshape(Bs, N_MEM, D), st_in, w,
                              B=Bs, L=Ls, last_g=last_g)
        s_st.append(st)

    y_prompt = outp.reshape(Bp, Lp, D)
    y_sample = outs.reshape(Bs, Ls, D)
    p_states = [jnp.stack([st[i] for st in p_st]) for i in range(6)]
    s_states = [jnp.stack([st[i] for st in s_st]) for i in range(6)]
    return (y_prompt, y_sample, jnp.stack(p_mk), jnp.stack(p_mv), *p_states, *s_states)
```

```python
import functools
import math

import jax
import jax.numpy as jnp
from jax import lax
from jax.experimental import pallas as pl
from jax.experimental.pallas import tpu as pltpu

F32 = jnp.float32
BF16 = jnp.bfloat16

D_MODEL = 1024
DEPTH = 4
N_MEM = 256
X_HEADS = 4
X_DH = D_MODEL // X_HEADS
M_HEADS = 4
M_DH = D_MODEL // M_HEADS
R_WIDTH = D_MODEL
R_BLOCKS = 8
R_BDIM = R_WIDTH // R_BLOCKS
R_C = 8.0
G_HEADS = 8
G_DH = D_MODEL // G_HEADS
N_BRANCH = 3
CONV_W = 4
CHUNK = 64
D_FF = 2816
EPS = 1e-6
M_INIT = -1e30

CONV_CH = R_WIDTH + 3 * D_MODEL
OFF_MQKV = CONV_CH
OFF_GATES = OFF_MQKV + 3 * D_MODEL
OFF_SMALL = OFF_GATES + 3 * D_MODEL
N_SMALL = 2 * M_HEADS + 2 * G_HEADS
SMALL_PAD = 32
SM_MI, SM_MF = 0, M_HEADS
SM_GB_BLK, SM_GA_BLK = 1, 2

LANE = 128
SUBLANE = 8
VMEM_LIMIT = 52 * 1024 * 1024


def _cparams(n_axes, vmem=VMEM_LIMIT):
    return pltpu.CompilerParams(dimension_semantics=("arbitrary",) * n_axes,
                                vmem_limit_bytes=vmem)


def _resident(shape, index_map):
    return pl.BlockSpec(shape, index_map, pipeline_mode=pl.Buffered(1))


_SMEM = pl.BlockSpec(memory_space=pltpu.SMEM)
_ANY = pl.BlockSpec(memory_space=pl.ANY)


def _rms(x, g):
    return x * lax.rsqrt(jnp.mean(x * x, axis=-1, keepdims=True) + EPS) * g


def _softplus(x):
    return jnp.maximum(x, 0.0) + jnp.log1p(jnp.exp(-jnp.abs(x)))


def _l2norm(x):
    return x * lax.rsqrt(jnp.sum(x * x, axis=-1, keepdims=True) + EPS)


def _silu(x):
    return x * jax.nn.sigmoid(x)


def _dot(a, b):
    return jnp.dot(a, b, preferred_element_type=F32)


def _dot_nt(a, b):
    return lax.dot_general(a, b, (((1,), (1,)), ((), ())), preferred_element_type=F32)


def _dot_tn(a, b):
    return lax.dot_general(a, b, (((0,), (0,)), ((), ())), preferred_element_type=F32)


def _mm_body(x_ref, w_ref, o_ref):
    o_ref[...] = _dot(x_ref[...], w_ref[...]).astype(o_ref.dtype)


def matmul(x, w, out_dtype, tm, tn):
    M, K = x.shape
    N = w.shape[1]
    tm = min(tm, M)
    return pl.pallas_call(
        _mm_body,
        grid=(N // tn, M // tm),
        in_specs=[pl.BlockSpec((tm, K), lambda j, i: (i, 0)),
                  pl.BlockSpec((K, tn), lambda j, i: (0, j))],
        out_specs=pl.BlockSpec((tm, tn), lambda j, i: (i, j)),
        out_shape=jax.ShapeDtypeStruct((M, N), out_dtype),
        compiler_params=_cparams(2),
        name="matmul",
    )(x, w)


def _mm_t_body(w_ref, x_ref, o_ref):
    o_ref[...] = _dot_nt(w_ref[...], x_ref[...])


def matmul_t(wt, x, tm):
    P, K = wt.shape
    M = x.shape[0]
    tm = min(tm, M)
    return pl.pallas_call(
        _mm_t_body,
        grid=(M // tm,),
        in_specs=[pl.BlockSpec((P, K), lambda i: (0, 0)),
                  pl.BlockSpec((tm, K), lambda i: (i, 0))],
        out_specs=pl.BlockSpec((P, tm), lambda i: (0, i)),
        out_shape=jax.ShapeDtypeStruct((P, M), F32),
        compiler_params=_cparams(1),
        name="matmul_t",
    )(wt, x)


def _inproj_conv_body(x_ref, w_ref, c0_ref, cw_ref, cb_ref, o_ref, cn_ref, s_ref, *,
                      bt, Lt, seq_tiles):
    tn = w_ref.shape[1]
    pad = SUBLANE
    lo = pad - (CONV_W - 1)
    pc = _dot(x_ref[...], w_ref[...]).reshape(bt, Lt, tn)
    if seq_tiles == 1:
        s_ref[:, lo:pad, :] = c0_ref[...]
    else:
        first = (pl.program_id(1) % seq_tiles) == 0

        @pl.when(first)
        def _():
            s_ref[:, lo:pad, :] = c0_ref[...]

        @pl.when(jnp.logical_not(first))
        def _():
            s_ref[:, lo:pad, :] = s_ref[:, Lt + lo:Lt + pad, :]

    s_ref[:, pad:pad + Lt, :] = pc
    acc = cb_ref[...][None] + pc * cw_ref[CONV_W - 1:CONV_W, :][None]
    for j in range(CONV_W - 1):
        acc = acc + s_ref[:, lo + j:lo + j + Lt, :] * cw_ref[j:j + 1, :][None]
    o_ref[...] = acc.reshape(bt * Lt, tn)
    cn_ref[...] = s_ref[:, Lt + lo:Lt + pad, :]


def inproj_conv(xn, w, conv0, cw, cb, *, B, L, tm=512, tn=2048):
    M, K = xn.shape
    tm = min(tm, M)
    if L >= tm:
        bt, Lt, seq_tiles = 1, tm, L // tm
        seq = lambda j, i: (i // seq_tiles, 0, j)
    else:
        bt, Lt, seq_tiles = tm // L, L, 1
        seq = lambda j, i: (i, 0, j)
    return pl.pallas_call(
        functools.partial(_inproj_conv_body, bt=bt, Lt=Lt, seq_tiles=seq_tiles),
        grid=(CONV_CH // tn, M // tm),
        in_specs=[pl.BlockSpec((tm, K), lambda j, i: (i, 0)),
                  pl.BlockSpec((K, tn), lambda j, i: (0, j)),
                  pl.BlockSpec((bt, CONV_W - 1, tn), seq),
                  pl.BlockSpec((CONV_W, tn), lambda j, i: (0, j)),
                  pl.BlockSpec((1, tn), lambda j, i: (0, j))],
        out_specs=[pl.BlockSpec((tm, tn), lambda j, i: (i, j)),
                   pl.BlockSpec((bt, CONV_W - 1, tn), seq)],
        out_shape=[jax.ShapeDtypeStruct((M, CONV_CH), F32),
                   jax.ShapeDtypeStruct((B, CONV_W - 1, CONV_CH), F32)],
        scratch_shapes=[pltpu.VMEM((bt, Lt + SUBLANE, tn), F32)],
        compiler_params=_cparams(2),
        name="inproj_conv",
    )(xn, w, conv0, cw, cb)


def _ffn_body(*refs, has_proj, norm_dtype):
    it = iter(refs)
    x_ref = next(it)
    if has_proj:
        o_ref, wo_ref = next(it), next(it)
    g_ref, wgu_ref, wd_ref = next(it), next(it), next(it)
    if norm_dtype is not None:
        g2_ref = next(it)
    xo_ref = next(it)
    if norm_dtype is not None:
        no_ref = next(it)
    x = x_ref[...]
    if has_proj:
        x = x + _dot(o_ref[...], wo_ref[...])
    xn = _rms(x, g_ref[...]).astype(BF16)
    gu = _dot(xn, wgu_ref[...])
    g = gu[:, :D_FF]
    u = gu[:, D_FF:]
    a = (g * jax.nn.sigmoid(g) * u).astype(BF16)
    y = x + 0.5 * _dot(a, wd_ref[...])
    xo_ref[...] = y
    if norm_dtype is not None:
        no_ref[...] = _rms(y, g2_ref[...]).astype(norm_dtype)


def ffn(x, norm_g, w_gu, w_d, *, proj=None, next_g=None, norm_dtype=None, tm=512):
    M, D = x.shape
    tm = min(tm, M)
    row = lambda i: (i, 0)
    fixed = lambda i: (0, 0)
    args = [x]
    in_specs = [pl.BlockSpec((tm, D), row)]
    if proj is not None:
        o, wo = proj
        args += [o, wo]
        in_specs += [pl.BlockSpec((tm, D), row), _resident((D, D), fixed)]
    args += [norm_g, w_gu, w_d]
    in_specs += [_resident((1, D), fixed), _resident(w_gu.shape, fixed), _resident(w_d.shape, fixed)]
    out_shape = [jax.ShapeDtypeStruct((M, D), F32)]
    out_specs = [pl.BlockSpec((tm, D), row)]
    if norm_dtype is not None:
        args.append(next_g)
        in_specs.append(_resident((1, D), fixed))
        out_shape.append(jax.ShapeDtypeStruct((M, D), norm_dtype))
        out_specs.append(pl.BlockSpec((tm, D), row))
    res = pl.pallas_call(
        functools.partial(_ffn_body, has_proj=proj is not None, norm_dtype=norm_dtype),
        grid=(M // tm,),
        in_specs=in_specs, out_specs=out_specs, out_shape=out_shape,
        compiler_params=_cparams(1),
        name="ffn",
    )(*args)
    return res if norm_dtype is not None else (res[0], None)


def _rglru_body(rx_ref, ry_ref, wa_ref, wx_ref, ba_ref, bx_ref, lam_ref, h0_ref,
                hr_ref, h1_ref, a_s, b_s, hc_s, *, B, tl):
    R = B * tl

    @pl.when(pl.program_id(1) == 0)
    def _():
        hc_s[...] = h0_ref[...]

    rx = rx_ref[...].reshape(R, R_BDIM)
    rxb = rx.astype(BF16)
    r = jax.nn.sigmoid(_dot(rxb, wa_ref[0]) + ba_ref[...])
    ig = jax.nn.sigmoid(_dot(rxb, wx_ref[0]) + bx_ref[...])
    log_a = -R_C * r * _softplus(-lam_ref[...])
    a = jnp.exp(log_a)
    a_s[...] = a
    b_s[...] = jnp.sqrt(jnp.tanh(-log_a) * (a * a + 1.0)) * (ig * rx)

    def step(t, h):
        rows = pl.ds(t, B, stride=tl)
        h = a_s[rows, :] * h + b_s[rows, :]
        a_s[rows, :] = h
        return h

    h = lax.fori_loop(0, tl, step, hc_s[...], unroll=SUBLANE)
    hc_s[...] = h
    h1_ref[...] = h
    hr = a_s[...] * jax.nn.gelu(ry_ref[...].reshape(R, R_BDIM))
    hr_ref[...] = hr.reshape(B, tl, R_BDIM).astype(hr_ref.dtype)


def rglru(conv3, gates3, wa, wx, ba, bx, lam, h0, *, tl, out_dtype):
    B, L, _ = conv3.shape
    ry_blk = D_MODEL // R_BDIM
    vec = lambda n, j: (0, n)
    return pl.pallas_call(
        functools.partial(_rglru_body, B=B, tl=tl),
        grid=(R_BLOCKS, L // tl),
        in_specs=[pl.BlockSpec((B, tl, R_BDIM), lambda n, j: (0, j, n)),
                  pl.BlockSpec((B, tl, R_BDIM), lambda n, j: (0, j, ry_blk + n)),
                  pl.BlockSpec((1, R_BDIM, R_BDIM), lambda n, j: (n, 0, 0)),
                  pl.BlockSpec((1, R_BDIM, R_BDIM), lambda n, j: (n, 0, 0)),
                  pl.BlockSpec((1, R_BDIM), vec),
                  pl.BlockSpec((1, R_BDIM), vec),
                  pl.BlockSpec((1, R_BDIM), vec),
                  pl.BlockSpec((B, R_BDIM), vec)],
        out_specs=[pl.BlockSpec((B, tl, R_BDIM), lambda n, j: (0, j, n)),
                   pl.BlockSpec((B, R_BDIM), vec)],
        out_shape=[jax.ShapeDtypeStruct((B, L, R_WIDTH), out_dtype),
                   jax.ShapeDtypeStruct((B, R_WIDTH), F32)],
        scratch_shapes=[pltpu.VMEM((B * tl, R_BDIM), F32), pltpu.VMEM((B * tl, R_BDIM), F32),
                        pltpu.VMEM((B, R_BDIM), F32)],
        compiler_params=_cparams(2),
        name="rglru",
    )(conv3, gates3, wa, wx, ba, bx, lam, h0)


def _chunk_masks(c):
    t = lax.broadcasted_iota(jnp.int32, (c, c), 0)
    s = lax.broadcasted_iota(jnp.int32, (c, c), 1)
    return s <= t, s == t, s < t


def _col_from_row(row, eye):
    return jnp.sum(jnp.where(eye, row, 0.0), axis=1, keepdims=True)


def _row_from_col(col, eye):
    return jnp.sum(jnp.where(eye, col, 0.0), axis=0, keepdims=True)


def _split_iter(it, ncl):
    if ncl == 1:
        return it, 0
    return it // ncl, it % ncl


def _chunk_rows(ci, c):
    if isinstance(ci, int):
        return pl.ds(ci * c, c)
    return pl.ds(pl.multiple_of(ci * c, c), c)


def _mlstm_body(*refs, bt, bu, tl, c, aliased, carried):
    (q_ref, k_ref, v_ref, mo_ref, sm_ref, c0_ref, n0_ref, m0_ref, ib_ref, fb_ref, g_ref) = refs[:11]
    hm_ref, c1_ref, n1_ref, m1_ref = refs[11 + int(aliased):15 + int(aliased)]
    ncl = tl // c
    tri, eye, _ = _chunk_masks(c)
    scale = M_DH ** -0.5
    nch = bu * M_HEADS
    if carried:
        scr = refs[15 + int(aliased):]
        c_s, n_s, m_s = scr[:nch], scr[nch:2 * nch], scr[2 * nch:]
        j = pl.program_id(1)

        @pl.when(j == 0)
        def _():
            for bb in range(bu):
                for h in range(M_HEADS):
                    c_s[bb * M_HEADS + h][...] = c0_ref[bb, h]
                    n_s[bb * M_HEADS + h][...] = n0_ref[bb, h]
                    m_s[bb * M_HEADS + h][...] = m0_ref[bb, h]

    def chunk(it, carry):
        bg, ci = _split_iter(it, ncl)
        rows = _chunk_rows(ci, c)
        chains = [(bb, h) for bb in range(bu) for h in range(M_HEADS)]
        st = []
        for bb, h in chains:
            bi = bg * bu + bb
            ch = bb * M_HEADS + h
            cols = slice(h * M_DH, (h + 1) * M_DH)
            qc = q_ref[bi, rows, cols].astype(BF16)
            kf = k_ref[bi, rows, cols].astype(F32) * scale
            vc = v_ref[bi, rows, cols].astype(BF16)
            if carried:
                C, n, m = c_s[ch][...], n_s[ch][...], m_s[ch][:, 0:1]
            else:
                C, n, m = c0_ref[bi, h], n0_ref[bi, h], m0_ref[bi, h][:, 0:1]
            li_row = sm_ref[SM_MI + h, bi, pl.ds(ci, 1), :] + ib_ref[h]
            lf_row = -_softplus(-(sm_ref[SM_MF + h, bi, pl.ds(ci, 1), :] + fb_ref[h]))
            b_col = jnp.sum(jnp.where(tri, lf_row, 0.0), axis=1, keepdims=True)
            b_row = _row_from_col(b_col, eye)
            li_col = _col_from_row(li_row, eye)
            logD = jnp.where(tri, b_col - b_row + li_row, -jnp.inf)
            inter = b_col + m
            mt = jnp.maximum(inter, jnp.max(logD, axis=1, keepdims=True))
            m_new = mt[c - 1:c, :]
            b_last = b_col[c - 1:c, :]
            kw = kf * jnp.exp(b_last - b_col + li_col - m_new)
            dec = jnp.exp(b_last + m - m_new)
            qk = _dot_nt(qc, kf.astype(BF16))
            qC = _dot(qc, C.astype(BF16))
            C_new = dec * C + _dot_tn(kw.astype(BF16), vc)
            n_new = dec * n + jnp.sum(kw, axis=0, keepdims=True)
            m_row = jnp.broadcast_to(m_new, (1, LANE))
            if carried:
                c_s[ch][...], n_s[ch][...], m_s[ch][...] = C_new, n_new, m_row
            else:
                c1_ref[bi, h], n1_ref[bi, h], m1_ref[bi, h] = C_new, n_new, m_row
            st.append(dict(
                bi=bi, cols=cols, vc=vc, mt=mt, qk=qk, qC=qC,
                wi=jnp.exp(inter - mt), dmat=jnp.exp(logD - mt),
                qn=jnp.sum(qc.astype(F32) * n, axis=1, keepdims=True)))
        for d in st:
            bi, cols = d["bi"], d["cols"]
            s = d["qk"] * d["dmat"]
            num = _dot(s.astype(BF16), d["vc"]) + d["wi"] * d["qC"]
            den = jnp.sum(s, axis=1, keepdims=True) + d["wi"] * d["qn"]
            hh = num / jnp.maximum(jnp.abs(den), jnp.exp(-d["mt"]))
            y = _rms(hh, g_ref[:, cols]) * jax.nn.sigmoid(mo_ref[bi, rows, cols])
            hm_ref[bi, rows, cols] = y.astype(hm_ref.dtype)
        return carry

    lax.fori_loop(0, (bt // bu) * ncl, chunk, 0)

    if carried:
        @pl.when(j == pl.num_programs(1) - 1)
        def _():
            for bb in range(bu):
                for h in range(M_HEADS):
                    c1_ref[bb, h] = c_s[bb * M_HEADS + h][...]
                    n1_ref[bb, h] = n_s[bb * M_HEADS + h][...]
                    m1_ref[bb, h] = m_s[bb * M_HEADS + h][...]


def mlstm(mqkv3, gates3, small4, C0s, n0s, m0s, l_in, ib, fb, gain, C_prev, l, *, c, bt, bu, tl,
          out_dtype):
    B, L, _ = mqkv3.shape
    ncl = tl // c
    tok = lambda cb: pl.BlockSpec((bt, tl, D_MODEL), lambda i, j: (i, j, cb))
    st_in = lambda i, j: (l_in, i, 0, 0, 0)
    st_out = lambda i, j: (l, i, 0, 0, 0)
    st = lambda i, j: (i, 0, 0, 0)
    in_specs = [tok(0), tok(1), tok(2), tok(0),
                pl.BlockSpec((SUBLANE, bt, ncl, c), lambda i, j: (0, i, j, 0)),
                pl.BlockSpec((None, bt, M_HEADS, M_DH, M_DH), st_in),
                pl.BlockSpec((None, bt, M_HEADS, 1, M_DH), st_in),
                pl.BlockSpec((None, bt, M_HEADS, 1, LANE), st_in),
                _SMEM, _SMEM,
                pl.BlockSpec((1, D_MODEL), lambda i, j: (0, 0))]
    args = [mqkv3, mqkv3, mqkv3, gates3, small4, C0s, n0s, m0s, ib, fb, gain]
    aliases = {}
    if C_prev is not None:
        in_specs.append(_ANY)
        args.append(C_prev)
        aliases = {len(args) - 1: 1}
    carried = L > c
    scratch = []
    if carried:
        assert bt == bu
        nch = bu * M_HEADS
        scratch = ([pltpu.VMEM((M_DH, M_DH), F32)] * nch + [pltpu.VMEM((1, M_DH), F32)] * nch
                   + [pltpu.VMEM((1, LANE), F32)] * nch)
    return pl.pallas_call(
        functools.partial(_mlstm_body, bt=bt, bu=bu, tl=tl, c=c, aliased=C_prev is not None,
                          carried=carried),
        grid=(B // bt, L // tl),
        scratch_shapes=scratch,
        in_specs=in_specs,
        out_specs=[pl.BlockSpec((bt, tl, D_MODEL), lambda i, j: (i, j, 0)),
                   pl.BlockSpec((None, bt, M_HEADS, M_DH, M_DH), st_out),
                   pl.BlockSpec((bt, M_HEADS, 1, M_DH), st),
                   pl.BlockSpec((bt, M_HEADS, 1, LANE), st)],
        out_shape=[jax.ShapeDtypeStruct((B, L, D_MODEL), out_dtype),
                   jax.ShapeDtypeStruct((DEPTH, B, M_HEADS, M_DH, M_DH), F32),
                   jax.ShapeDtypeStruct((B, M_HEADS, 1, M_DH), F32),
                   jax.ShapeDtypeStruct((B, M_HEADS, 1, LANE), F32)],
        input_output_aliases=aliases,
        compiler_params=_cparams(2),
        name="mlstm",
    )(*args)


def _gdn_prep_body(q_ref, k_ref, v_ref, sb_ref, sa_ref, alog_ref, dtb_ref,
                   lm_ref, qk_ref, vb_ref, kbg_ref, qe_ref, kd_ref, eg_ref, *, bt, tl, c):
    ncl = tl // c
    tri, eye, strict = _chunk_masks(c)

    def chunk(it, carry):
        bi, ci = _split_iter(it, ncl)
        rows = _chunk_rows(ci, c)
        crow = pl.ds(ci, 1)
        for h in range(G_HEADS):
            cols = slice(h * G_DH, (h + 1) * G_DH)
            q = _l2norm(_silu(q_ref[bi, rows, cols])) * (G_DH ** -0.5)
            k = _l2norm(_silu(k_ref[bi, rows, cols]))
            v = _silu(v_ref[bi, rows, cols])
            beta_row = jax.nn.sigmoid(sb_ref[h, bi, crow, :])
            g_row = -jnp.exp(alog_ref[h]) * _softplus(sa_ref[h, bi, crow, :] + dtb_ref[h])
            G_col = jnp.sum(jnp.where(tri, g_row, 0.0), axis=1, keepdims=True)
            G_row = _row_from_col(G_col, eye)
            beta_col = _col_from_row(beta_row, eye)
            dec = jnp.exp(jnp.where(tri, G_col - G_row, -jnp.inf))
            kb16 = k.astype(BF16)
            kk = _dot_nt(kb16, kb16)
            lm_ref[h, bi, rows, :] = jnp.where(strict, beta_col * kk * dec, 0.0)
            qk_ref[h, bi, rows, :] = (_dot_nt(q.astype(BF16), kb16) * dec).astype(qk_ref.dtype)
            eG = jnp.exp(G_col)
            vb_ref[bi, rows, cols] = (v * beta_col).astype(vb_ref.dtype)
            kbg_ref[bi, rows, cols] = (k * beta_col * eG).astype(kbg_ref.dtype)
            qe_ref[bi, rows, cols] = (q * eG).astype(qe_ref.dtype)
            gl = G_col[c - 1:c, :]
            kd_ref[bi, rows, cols] = (k * jnp.exp(gl - G_col)).astype(kd_ref.dtype)
            eg_ref[h, bi, crow, :] = jnp.exp(G_row)
        return carry

    lax.fori_loop(0, bt * ncl, chunk, 0)


def gdn_prep(conv3, small4, alog, dtb, *, c, bt, tl, idt):
    B, L, _ = conv3.shape
    ncl = tl // c
    tok_in = lambda cb: pl.BlockSpec((bt, tl, D_MODEL), lambda i, j: (i, j, cb))
    tok = pl.BlockSpec((bt, tl, D_MODEL), lambda i, j: (i, j, 0))
    small = lambda blk: pl.BlockSpec((SUBLANE, bt, ncl, c), lambda i, j: (blk, i, j, 0))
    per_head = pl.BlockSpec((G_HEADS, bt, tl, c), lambda i, j: (0, i, j, 0))
    return pl.pallas_call(
        functools.partial(_gdn_prep_body, bt=bt, tl=tl, c=c),
        grid=(B // bt, L // tl),
        in_specs=[tok_in(1), tok_in(2), tok_in(3), small(SM_GB_BLK), small(SM_GA_BLK), _SMEM, _SMEM],
        out_specs=[per_head, per_head, tok, tok, tok, tok, small(0)],
        out_shape=[jax.ShapeDtypeStruct((G_HEADS, B, L, c), F32),
                   jax.ShapeDtypeStruct((G_HEADS, B, L, c), idt),
                   jax.ShapeDtypeStruct((B, L, D_MODEL), idt),
                   jax.ShapeDtypeStruct((B, L, D_MODEL), idt),
                   jax.ShapeDtypeStruct((B, L, D_MODEL), idt),
                   jax.ShapeDtypeStruct((B, L, D_MODEL), idt),
                   jax.ShapeDtypeStruct((G_HEADS, B, L // c, c), F32)],
        compiler_params=_cparams(2),
        name="gdn_prep",
    )(conv3, conv3, conv3, small4, small4, alog, dtb)


def _fsub_body(l_ref, x_ref, *, c):
    jrow = lax.broadcasted_iota(jnp.int32, (c, LANE), 0)
    x_ref[...] = jnp.zeros(x_ref.shape, F32)

    def t_body(t, carry):
        def sb_body(sb, acc):
            s0 = pl.multiple_of(sb * SUBLANE, SUBLANE)
            lblk = l_ref[t, pl.ds(s0, SUBLANE), :]
            for kk in range(SUBLANE):
                acc = acc - lblk[kk:kk + 1, :] * x_ref[s0 + kk]
            return acc

        acc0 = jnp.where(jrow == t, 1.0, 0.0)
        x_ref[t] = lax.fori_loop(0, (t + SUBLANE - 1) // SUBLANE, sb_body, acc0)
        return carry

    lax.fori_loop(0, c, t_body, 0)


def tri_inverse(lt):
    c, _, n = lt.shape
    blk = pl.BlockSpec((c, c, LANE), lambda i: (0, 0, i))
    return pl.pallas_call(
        functools.partial(_fsub_body, c=c),
        grid=(n // LANE,),
        in_specs=[blk], out_specs=blk,
        out_shape=jax.ShapeDtypeStruct((c, c, n), F32),
        compiler_params=_cparams(1),
        name="tri_inverse",
    )(lt)


def _gdn_rec_body(*refs, bt, tl, c, aliased, carried):
    (t_ref, qk_ref, vb_ref, kbg_ref, qe_ref, kd_ref, eg_ref, gz_ref, s0_ref, g_ref) = refs[:10]
    hd_ref, s1_ref = refs[10 + int(aliased):12 + int(aliased)]
    ncl = tl // c
    gain = g_ref[...]
    if carried:
        s_s = refs[12 + int(aliased):]
        j = pl.program_id(1)

        @pl.when(j == 0)
        def _():
            for h in range(G_HEADS):
                s_s[h][...] = s0_ref[0, h]

    def chunk(it, carry):
        bi, ci = _split_iter(it, ncl)
        rows = _chunk_rows(ci, c)
        heads = [slice(h * G_DH, (h + 1) * G_DH) for h in range(G_HEADS)]
        uw = []
        for h, cols in enumerate(heads):
            T = t_ref[h, bi, rows, :].astype(BF16)
            rhs = jnp.concatenate([vb_ref[bi, rows, cols], kbg_ref[bi, rows, cols]], axis=1)
            uw.append(_dot(T, rhs.astype(BF16)))
        ws = []
        for h, cols in enumerate(heads):
            S = s_s[h][...] if carried else s0_ref[bi, h]
            lhs = jnp.concatenate([uw[h][:, G_DH:].astype(qe_ref.dtype), qe_ref[bi, rows, cols]],
                                  axis=0)
            ws.append(_dot(lhs.astype(BF16), S.astype(BF16)))
        for h, cols in enumerate(heads):
            u = (uw[h][:, :G_DH] - ws[h][:c]).astype(BF16)
            o = ws[h][c:] + _dot(qk_ref[h, bi, rows, :].astype(BF16), u)
            dS = _dot_tn(kd_ref[bi, rows, cols].astype(BF16), u)
            egl = eg_ref[h, bi, pl.ds(ci, 1), :][:, c - 1:c]
            if carried:
                s_s[h][...] = egl * s_s[h][...] + dS
            else:
                s1_ref[bi, h] = egl * s0_ref[bi, h] + dS
            y = _rms(o, gain) * _silu(gz_ref[bi, rows, cols])
            hd_ref[bi, rows, cols] = y.astype(hd_ref.dtype)
        return carry

    lax.fori_loop(0, bt * ncl, chunk, 0)

    if carried:
        @pl.when(j == pl.num_programs(1) - 1)
        def _():
            for h in range(G_HEADS):
                s1_ref[0, h] = s_s[h][...]


def gdn_rec(T, qk, vb, kbg, qe, kd, eg, gates3, S0s, l_in, gain, S_prev, l, *, c, bt, tl, out_dtype):
    B, L, _ = vb.shape
    ncl = tl // c
    tok = pl.BlockSpec((bt, tl, D_MODEL), lambda i, j: (i, j, 0))
    per_head = pl.BlockSpec((G_HEADS, bt, tl, c), lambda i, j: (0, i, j, 0))
    in_specs = [per_head, per_head, tok, tok, tok, tok,
                pl.BlockSpec((G_HEADS, bt, ncl, c), lambda i, j: (0, i, j, 0)),
                pl.BlockSpec((bt, tl, D_MODEL), lambda i, j: (i, j, 2)),
                pl.BlockSpec((None, bt, G_HEADS, G_DH, G_DH), lambda i, j: (l_in, i, 0, 0, 0)),
                pl.BlockSpec((1, G_DH), lambda i, j: (0, 0))]
    args = [T, qk, vb, kbg, qe, kd, eg, gates3, S0s, gain]
    aliases = {}
    if S_prev is not None:
        in_specs.append(_ANY)
        args.append(S_prev)
        aliases = {len(args) - 1: 1}
    carried = L > c
    if carried:
        assert bt == 1
    scratch = [pltpu.VMEM((G_DH, G_DH), F32)] * G_HEADS if carried else []
    return pl.pallas_call(
        functools.partial(_gdn_rec_body, bt=bt, tl=tl, c=c, aliased=S_prev is not None,
                          carried=carried),
        grid=(B // bt, L // tl),
        scratch_shapes=scratch,
        in_specs=in_specs,
        out_specs=[tok, pl.BlockSpec((None, bt, G_HEADS, G_DH, G_DH), lambda i, j: (l, i, 0, 0, 0))],
        out_shape=[jax.ShapeDtypeStruct((B, L, D_MODEL), out_dtype),
                   jax.ShapeDtypeStruct((DEPTH, B, G_HEADS, G_DH, G_DH), F32)],
        input_output_aliases=aliases,
        compiler_params=_cparams(2),
        name="gdn_rec",
    )(*args)


def _merge_body(x_ref, xn_ref, hm_ref, hr_ref, hd_ref, wg_ref, bg_ref, wb_ref, wo_ref,
                gx_ref, wq_ref, xo_ref, q_ref):
    xn = xn_ref[...]
    merged = None
    for n, br in enumerate((hm_ref, hr_ref, hd_ref)):
        lo = n * D_MODEL
        gate = jax.nn.sigmoid(_dot(xn, wg_ref[:, lo:lo + D_MODEL]) + bg_ref[:, lo:lo + D_MODEL])
        term = gate * _dot(br[...].astype(BF16), wb_ref[n])
        merged = term if merged is None else merged + term
    x = x_ref[...] + _dot(merged.astype(BF16), wo_ref[...])
    xo_ref[...] = x
    q_ref[...] = _dot(_rms(x, gx_ref[...]).astype(BF16), wq_ref[...]).astype(q_ref.dtype)


def merge(x, xn, hm, hr, hd, w_gate, b_gate, w_branch, w_out, xa_g, xa_wq, *, tm=512):
    M, D = x.shape
    tm = min(tm, M)
    row = pl.BlockSpec((tm, D), lambda i: (i, 0))
    f2 = lambda i: (0, 0)
    return pl.pallas_call(
        _merge_body,
        grid=(M // tm,),
        in_specs=[row, row, row, row, row,
                  _resident(w_gate.shape, f2), _resident(b_gate.shape, f2),
                  _resident(w_branch.shape, lambda i: (0, 0, 0)), _resident(w_out.shape, f2),
                  _resident(xa_g.shape, f2), _resident(xa_wq.shape, f2)],
        out_specs=[row, row],
        out_shape=[jax.ShapeDtypeStruct((M, D), F32), jax.ShapeDtypeStruct((M, D), F32)],
        compiler_params=_cparams(1),
        name="merge",
    )(x, xn, hm, hr, hd, w_gate, b_gate, w_branch, w_out, xa_g, xa_wq)


def _xattn_body(q_ref, k_ref, v_ref, o_ref, *, bt, tq):
    scale = X_DH ** -0.5
    rows = []
    for bi in range(bt):
        q = q_ref[bi * tq:(bi + 1) * tq, :]
        heads = []
        for h in range(X_HEADS):
            lo = h * X_DH
            qh = q[:, lo:lo + X_DH].astype(BF16)
            kh = k_ref[bi, :, lo:lo + X_DH].astype(BF16)
            vh = v_ref[bi, :, lo:lo + X_DH].astype(BF16)
            s = _dot_nt(qh, kh) * scale
            e = jnp.exp(s - jnp.max(s, axis=-1, keepdims=True))
            p = e / jnp.sum(e, axis=-1, keepdims=True)
            heads.append(_dot(p.astype(BF16), vh))
        rows.append(jnp.concatenate(heads, axis=1))
    o = rows[0] if bt == 1 else jnp.concatenate(rows, axis=0)
    o_ref[...] = o.astype(o_ref.dtype)


def xattn(q, mem_k, mem_v, lk, kcol, vcol, *, B, L, bt, tq):
    nq = L // tq
    R = bt * tq
    kv = lambda cb: pl.BlockSpec((None, bt, N_MEM, D_MODEL), lambda i, j: (lk, i, 0, cb))
    qo = pl.BlockSpec((R, D_MODEL), lambda i, j: (i * nq + j, 0))
    return pl.pallas_call(
        functools.partial(_xattn_body, bt=bt, tq=tq),
        grid=(B // bt, nq),
        in_specs=[qo, kv(kcol), kv(vcol)],
        out_specs=qo,
        out_shape=jax.ShapeDtypeStruct((B * L, D_MODEL), BF16),
        compiler_params=_cparams(2),
        name="xattn",
    )(q, mem_k, mem_v)


def _group_cfg(B, L):
    c = math.gcd(CHUNK, L)
    if L >= CHUNK:
        return dict(c=c, tl=min(512, L), m_bt=2, m_bu=2, g_bt=1, xa_bt=1, xa_tq=min(512, L),
                    idt=BF16)
    return dict(c=c, tl=L, m_bt=8, m_bu=2, g_bt=8, xa_bt=4, xa_tq=L, idt=F32)


def _layer(x, kv, state_in, l_in, prev, l, w, *, B, L, last_g):
    cfg = _group_cfg(B, L)
    c, tl, idt = cfg["c"], cfg["tl"], cfg["idt"]
    M = B * L
    conv0s, C0s, n0s, m0s, h0s, S0s = state_in
    C_prev, S_prev = prev if prev is not None else (None, None)

    x1, xn = ffn(x, w["f1_n"], w["f1_gu"], w["f1_d"], next_g=w["mix_n"], norm_dtype=BF16)

    conv2, conv_new = inproj_conv(xn, w["w_in_conv"], conv0s[l_in], w["conv_w"], w["conv_b"], B=B, L=L)
    mqkv = matmul(xn, w["w_in_mqkv"], idt, tm=1024, tn=1536)
    gates = matmul(xn, w["w_in_gates"], F32, tm=1024, tn=1536)
    small4 = matmul_t(w["w_in_small_t"], xn, tm=512).reshape(SMALL_PAD, B, L // c, c)
    conv3 = conv2.reshape(B, L, CONV_CH)
    mqkv3 = mqkv.reshape(B, L, 3 * D_MODEL)
    gates3 = gates.reshape(B, L, 3 * D_MODEL)

    hr, h1 = rglru(conv3, gates3, w["r_wa"], w["r_wx"], w["r_ba"], w["r_bx"], w["r_lam"],
                   h0s[l_in], tl=tl, out_dtype=idt)

    hm, C_out, n1, m1b = mlstm(mqkv3, gates3, small4, C0s, n0s, m0s, l_in, w["m_ib"], w["m_fb"],
                               w["m_norm"], C_prev, l, c=c, bt=cfg["m_bt"], bu=cfg["m_bu"], tl=tl,
                               out_dtype=idt)

    lm, qk, vb, kbg, qe, kd, eg = gdn_prep(conv3, small4, w["g_alog"], w["g_dtb"],
                                           c=c, bt=cfg["g_bt"], tl=tl, idt=idt)
    nchunks = M // c
    lt = lm.reshape(G_HEADS, nchunks, c, c).transpose(2, 3, 0, 1).reshape(c, c, G_HEADS * nchunks)
    tt = tri_inverse(lt)
    T = tt.reshape(c, c, G_HEADS, nchunks).transpose(2, 3, 0, 1).reshape(G_HEADS, B, L, c).astype(idt)
    hd, S_out = gdn_rec(T, qk, vb, kbg, qe, kd, eg, gates3, S0s, l_in, w["g_norm"], S_prev, l,
                        c=c, bt=cfg["g_bt"], tl=tl, out_dtype=idt)

    x2, q = merge(x1, xn, hm.reshape(M, D_MODEL), hr.reshape(M, D_MODEL), hd.reshape(M, D_MODEL),
                  w["w_gate"], w["b_gate"], w["w_branch"], w["w_out"], w["xa_n"], w["xa_wq"])
    mem_k, mem_v, lk, kcol, vcol = kv
    o = xattn(q, mem_k, mem_v, lk, kcol, vcol, B=B, L=L, bt=cfg["xa_bt"], tq=cfg["xa_tq"])
    if last_g is None:
        x3, y = ffn(x2, w["f2_n"], w["f2_gu"], w["f2_d"], proj=(o, w["xa_wo"]))
    else:
        x3, y = ffn(x2, w["f2_n"], w["f2_gu"], w["f2_d"], proj=(o, w["xa_wo"]),
                    next_g=last_g, norm_dtype=F32)
    small_state = (conv_new, n1.reshape(B, M_HEADS, M_DH), m1b[:, :, 0, 0], h1)
    return x3, y, small_state, (C_out, S_out)


def _prep_weights(l, ffn1_norm, ffn1_w_gu, ffn1_w_down, mix_norm, w_in, conv_w, conv_b, m_igate_b,
                  m_fgate_b, m_norm, r_wa, r_ba, r_wx, r_bx, r_lambda, g_a_log, g_dt_bias, g_norm,
                  w_gate, b_gate, w_branch, w_out, xa_norm, xa_wq, xa_wo, ffn2_norm, ffn2_w_gu,
                  ffn2_w_down):
    row = lambda a: a[l].reshape(1, -1).astype(F32)
    b16 = lambda a: a[l].astype(BF16)
    small_t = jnp.zeros((SMALL_PAD, D_MODEL), BF16).at[:N_SMALL].set(
        w_in[l][:, OFF_SMALL:OFF_SMALL + N_SMALL].T.astype(BF16))
    return dict(
        f1_n=row(ffn1_norm), f1_gu=b16(ffn1_w_gu), f1_d=b16(ffn1_w_down), mix_n=row(mix_norm),
        w_in_conv=w_in[l][:, :CONV_CH].astype(BF16),
        w_in_mqkv=w_in[l][:, OFF_MQKV:OFF_GATES].astype(BF16),
        w_in_gates=w_in[l][:, OFF_GATES:OFF_SMALL].astype(BF16),
        w_in_small_t=small_t,
        conv_w=conv_w[l].astype(F32), conv_b=row(conv_b),
        m_ib=m_igate_b[l].astype(F32), m_fb=m_fgate_b[l].astype(F32), m_norm=row(m_norm),
        r_wa=b16(r_wa), r_wx=b16(r_wx), r_ba=row(r_ba), r_bx=row(r_bx), r_lam=row(r_lambda),
        g_alog=g_a_log[l].astype(F32), g_dtb=g_dt_bias[l].astype(F32), g_norm=row(g_norm),
        w_gate=b16(w_gate), b_gate=row(b_gate), w_branch=b16(w_branch), w_out=b16(w_out),
        xa_n=row(xa_norm), xa_wq=b16(xa_wq), xa_wo=b16(xa_wo),
        f2_n=row(ffn2_norm), f2_gu=b16(ffn2_w_gu), f2_d=b16(ffn2_w_down))


def _stacked_state(conv, C, n, m, h, S):
    Dp, B = C.shape[:2]
    return (conv, C, n.reshape(Dp, B, M_HEADS, 1, M_DH),
            jnp.broadcast_to(m[..., None, None], (Dp, B, M_HEADS, 1, LANE)), h, S)


def kernel(x_prompt, x_sample, cache_mem_k, cache_mem_v, state_conv, state_mlstm_C, state_mlstm_n, state_mlstm_m, state_rglru_h, state_delta_S, mem_prompt, ffn1_norm, ffn1_w_gu, ffn1_w_down, mix_norm, w_in, conv_w, conv_b, m_igate_b, m_fgate_b, m_norm, r_wa, r_ba, r_wx, r_bx, r_lambda, g_a_log, g_dt_bias, g_norm, w_gate, b_gate, w_branch, w_out, xa_norm, xa_wq, xa_wk, xa_wv, xa_wo, ffn2_norm, ffn2_w_gu, ffn2_w_down, final_norm):
    Bp, Lp, D = x_prompt.shape
    Bs, Ls, _ = x_sample.shape
    stacked = (ffn1_norm, ffn1_w_gu, ffn1_w_down, mix_norm, w_in, conv_w, conv_b, m_igate_b,
               m_fgate_b, m_norm, r_wa, r_ba, r_wx, r_bx, r_lambda, g_a_log, g_dt_bias, g_norm,
               w_gate, b_gate, w_branch, w_out, xa_norm, xa_wq, xa_wo, ffn2_norm, ffn2_w_gu,
               ffn2_w_down)
    fin = final_norm.reshape(1, D).astype(F32)
    memp = mem_prompt.reshape(Bp * N_MEM, D).astype(BF16)
    p_state = _stacked_state(jnp.zeros((1, Bp, CONV_W - 1, CONV_CH), F32),
                             jnp.zeros((1, Bp, M_HEADS, M_DH, M_DH), F32),
                             jnp.zeros((1, Bp, M_HEADS, M_DH), F32),
                             jnp.full((1, Bp, M_HEADS), M_INIT, F32),
                             jnp.zeros((1, Bp, R_WIDTH), F32),
                             jnp.zeros((1, Bp, G_HEADS, G_DH, G_DH), F32))
    s_state = _stacked_state(state_conv, state_mlstm_C, state_mlstm_n, state_mlstm_m,
                             state_rglru_h, state_delta_S)
    ck = cache_mem_k.reshape(DEPTH, Bs, N_MEM, D).astype(BF16)
    cv = cache_mem_v.reshape(DEPTH, Bs, N_MEM, D).astype(BF16)

    yp = x_prompt.reshape(Bp * Lp, D)
    ys = x_sample.reshape(Bs * Ls, D)
    p_mk, p_mv, p_small, s_small = [], [], [], []
    p_big = s_big = None
    outp = outs = None
    for l in range(DEPTH):
        w = _prep_weights(l, *stacked)
        last_g = fin if l == DEPTH - 1 else None
        wkv = jnp.concatenate([xa_wk[l], xa_wv[l]], axis=1).astype(BF16)
        mkv = matmul(memp, wkv, F32, tm=512, tn=1024)
        p_mk.append(mkv[:, :D].reshape(Bp, N_MEM, X_HEADS, X_DH))
        p_mv.append(mkv[:, D:].reshape(Bp, N_MEM, X_HEADS, X_DH))
        kv_p = (mkv.reshape(1, Bp, N_MEM, 2 * D), mkv.reshape(1, Bp, N_MEM, 2 * D), 0, 0, 1)
        yp, outp, sm, p_big = _layer(yp, kv_p, p_state, 0, p_big, l, w, B=Bp, L=Lp, last_g=last_g)
        p_small.append(sm)
        ys, outs, sm, s_big = _layer(ys, (ck, cv, l, 0, 0), s_state, l, s_big, l, w,
                                     B=Bs, L=Ls, last_g=last_g)
        s_small.append(sm)

    def assemble(small, big):
        conv, n, m, h = [jnp.stack([sm[i] for sm in small]) for i in range(4)]
        C, S = big
        return conv, C, n, m, h, S

    y_prompt = outp.reshape(Bp, Lp, D)
    y_sample = outs.reshape(Bs, Ls, D)
    return (y_prompt, y_sample, jnp.stack(p_mk), jnp.stack(p_mv),
            *assemble(p_small, p_big), *assemble(s_small, s_big))
```

```python
import functools
import math

import jax
import jax.numpy as jnp
from jax import lax
from jax.experimental import pallas as pl
from jax.experimental.pallas import tpu as pltpu

F32 = jnp.float32
BF16 = jnp.bfloat16

D_MODEL = 1024
DEPTH = 4
N_MEM = 256
X_HEADS = 4
X_DH = D_MODEL // X_HEADS
M_HEADS = 4
M_DH = D_MODEL // M_HEADS
R_WIDTH = D_MODEL
R_BLOCKS = 8
R_BDIM = R_WIDTH // R_BLOCKS
R_C = 8.0
G_HEADS = 8
G_DH = D_MODEL // G_HEADS
N_BRANCH = 3
CONV_W = 4
CHUNK = 64
D_FF = 2816
EPS = 1e-6
M_INIT = -1e30

CONV_CH = R_WIDTH + 3 * D_MODEL
OFF_MQKV = CONV_CH
OFF_GATES = OFF_MQKV + 3 * D_MODEL
OFF_SMALL = OFF_GATES + 3 * D_MODEL
N_SMALL = 2 * M_HEADS + 2 * G_HEADS
SMALL_PAD = 32
SM_MI, SM_MF = 0, M_HEADS
SM_GB_BLK, SM_GA_BLK = 1, 2

CONV_SUB = 512
LANE = 128
SUBLANE = 8
VMEM_LIMIT = 52 * 1024 * 1024


def _cparams(n_axes, vmem=VMEM_LIMIT):
    return pltpu.CompilerParams(dimension_semantics=("arbitrary",) * n_axes,
                                vmem_limit_bytes=vmem)


def _resident(shape, index_map):
    return pl.BlockSpec(shape, index_map, pipeline_mode=pl.Buffered(1))


_SMEM = pl.BlockSpec(memory_space=pltpu.SMEM)
_ANY = pl.BlockSpec(memory_space=pl.ANY)


def _rms(x, g):
    return x * lax.rsqrt(jnp.mean(x * x, axis=-1, keepdims=True) + EPS) * g


def _softplus(x):
    return jnp.maximum(x, 0.0) + jnp.log1p(jnp.exp(-jnp.abs(x)))


def _sigmoid_t(x):
    return 0.5 * jnp.tanh(0.5 * x) + 0.5


def _l2norm(x):
    return x * lax.rsqrt(jnp.sum(x * x, axis=-1, keepdims=True) + EPS)


def _silu(x):
    return x * jax.nn.sigmoid(x)


def _dot(a, b):
    return jnp.dot(a, b, preferred_element_type=F32)


def _dot_nt(a, b):
    return lax.dot_general(a, b, (((1,), (1,)), ((), ())), preferred_element_type=F32)


def _dot_tn(a, b):
    return lax.dot_general(a, b, (((0,), (0,)), ((), ())), preferred_element_type=F32)


def _mm_body(x_ref, w_ref, o_ref):
    o_ref[...] = _dot(x_ref[...], w_ref[...]).astype(o_ref.dtype)


def matmul(x, w, out_dtype, tm, tn):
    M, K = x.shape
    N = w.shape[1]
    tm = min(tm, M)
    return pl.pallas_call(
        _mm_body,
        grid=(N // tn, M // tm),
        in_specs=[pl.BlockSpec((tm, K), lambda j, i: (i, 0)),
                  pl.BlockSpec((K, tn), lambda j, i: (0, j))],
        out_specs=pl.BlockSpec((tm, tn), lambda j, i: (i, j)),
        out_shape=jax.ShapeDtypeStruct((M, N), out_dtype),
        compiler_params=_cparams(2),
        name="matmul",
    )(x, w)


def _mm_t_body(w_ref, x_ref, o_ref):
    o_ref[...] = _dot_nt(w_ref[...], x_ref[...])


def matmul_t(wt, x, tm):
    P, K = wt.shape
    M = x.shape[0]
    tm = min(tm, M)
    return pl.pallas_call(
        _mm_t_body,
        grid=(M // tm,),
        in_specs=[pl.BlockSpec((P, K), lambda i: (0, 0)),
                  pl.BlockSpec((tm, K), lambda i: (i, 0))],
        out_specs=pl.BlockSpec((P, tm), lambda i: (0, i)),
        out_shape=jax.ShapeDtypeStruct((P, M), F32),
        compiler_params=_cparams(1),
        name="matmul_t",
    )(wt, x)


def _inproj_conv_body(x_ref, w_ref, c0_ref, cw_ref, cb_ref, o_ref, cn_ref, s_ref, *,
                      bt, Lt, seq_tiles):
    tn = w_ref.shape[1]
    pad = SUBLANE
    lo = pad - (CONV_W - 1)
    if seq_tiles == 1:
        s_ref[:, lo:pad, :] = c0_ref[...]
    else:
        first = (pl.program_id(1) % seq_tiles) == 0

        @pl.when(first)
        def _():
            s_ref[:, lo:pad, :] = c0_ref[...]

        @pl.when(jnp.logical_not(first))
        def _():
            s_ref[:, lo:pad, :] = s_ref[:, Lt + lo:Lt + pad, :]

    x = x_ref[...]
    for c0 in range(0, tn, CONV_SUB):
        cs = slice(c0, c0 + CONV_SUB)
        pc = _dot(x, w_ref[:, cs]).reshape(bt, Lt, CONV_SUB)
        s_ref[:, pad:pad + Lt, cs] = pc
        acc = cb_ref[:, cs][None] + pc * cw_ref[CONV_W - 1:CONV_W, cs][None]
        for j in range(CONV_W - 1):
            acc = acc + s_ref[:, lo + j:lo + j + Lt, cs] * cw_ref[j:j + 1, cs][None]
        o_ref[:, cs] = acc.reshape(bt * Lt, CONV_SUB)
    cn_ref[...] = s_ref[:, Lt + lo:Lt + pad, :]


def inproj_conv(xn, w, conv0, cw, cb, *, B, L, tm=512, tn=2048):
    M, K = xn.shape
    tm = min(tm, M)
    if L >= tm:
        bt, Lt, seq_tiles = 1, tm, L // tm
        seq = lambda j, i: (i // seq_tiles, 0, j)
    else:
        bt, Lt, seq_tiles = tm // L, L, 1
        seq = lambda j, i: (i, 0, j)
    return pl.pallas_call(
        functools.partial(_inproj_conv_body, bt=bt, Lt=Lt, seq_tiles=seq_tiles),
        grid=(CONV_CH // tn, M // tm),
        in_specs=[pl.BlockSpec((tm, K), lambda j, i: (i, 0)),
                  pl.BlockSpec((K, tn), lambda j, i: (0, j)),
                  pl.BlockSpec((bt, CONV_W - 1, tn), seq),
                  pl.BlockSpec((CONV_W, tn), lambda j, i: (0, j)),
                  pl.BlockSpec((1, tn), lambda j, i: (0, j))],
        out_specs=[pl.BlockSpec((tm, tn), lambda j, i: (i, j)),
                   pl.BlockSpec((bt, CONV_W - 1, tn), seq)],
        out_shape=[jax.ShapeDtypeStruct((M, CONV_CH), F32),
                   jax.ShapeDtypeStruct((B, CONV_W - 1, CONV_CH), F32)],
        scratch_shapes=[pltpu.VMEM((bt, Lt + SUBLANE, tn), F32)],
        compiler_params=_cparams(2),
        name="inproj_conv",
    )(xn, w, conv0, cw, cb)


def _ffn_body(*refs, has_proj, norm_dtype):
    it = iter(refs)
    x_ref = next(it)
    if has_proj:
        o_ref, wo_ref = next(it), next(it)
    g_ref, wgu_ref, wd_ref = next(it), next(it), next(it)
    if norm_dtype is not None:
        g2_ref = next(it)
    xo_ref = next(it)
    if norm_dtype is not None:
        no_ref = next(it)
    x = x_ref[...]
    if has_proj:
        x = x + _dot(o_ref[...], wo_ref[...])
    xn = _rms(x, g_ref[...]).astype(BF16)
    gu = _dot(xn, wgu_ref[...])
    g = gu[:, :D_FF]
    u = gu[:, D_FF:]
    a = (g * jax.nn.sigmoid(g) * u).astype(BF16)
    y = x + 0.5 * _dot(a, wd_ref[...])
    xo_ref[...] = y
    if norm_dtype is not None:
        no_ref[...] = _rms(y, g2_ref[...]).astype(norm_dtype)


def ffn(x, norm_g, w_gu, w_d, *, proj=None, next_g=None, norm_dtype=None, tm=512):
    M, D = x.shape
    tm = min(tm, M)
    row = lambda i: (i, 0)
    fixed = lambda i: (0, 0)
    args = [x]
    in_specs = [pl.BlockSpec((tm, D), row)]
    if proj is not None:
        o, wo = proj
        args += [o, wo]
        in_specs += [pl.BlockSpec((tm, D), row), _resident((D, D), fixed)]
    args += [norm_g, w_gu, w_d]
    in_specs += [_resident((1, D), fixed), _resident(w_gu.shape, fixed), _resident(w_d.shape, fixed)]
    out_shape = [jax.ShapeDtypeStruct((M, D), F32)]
    out_specs = [pl.BlockSpec((tm, D), row)]
    if norm_dtype is not None:
        args.append(next_g)
        in_specs.append(_resident((1, D), fixed))
        out_shape.append(jax.ShapeDtypeStruct((M, D), norm_dtype))
        out_specs.append(pl.BlockSpec((tm, D), row))
    res = pl.pallas_call(
        functools.partial(_ffn_body, has_proj=proj is not None, norm_dtype=norm_dtype),
        grid=(M // tm,),
        in_specs=in_specs, out_specs=out_specs, out_shape=out_shape,
        compiler_params=_cparams(1),
        name="ffn",
    )(*args)
    return res if norm_dtype is not None else (res[0], None)


def _rglru_body(rx_ref, ry_ref, wa_ref, wx_ref, ba_ref, bx_ref, lam_ref, h0_ref,
                hr_ref, h1_ref, a_s, b_s, hc_s, *, B, tl):
    R = B * tl

    @pl.when(pl.program_id(1) == 0)
    def _():
        hc_s[...] = h0_ref[...]

    rx = rx_ref[...].reshape(R, R_BDIM)
    rxb = rx.astype(BF16)
    r = _sigmoid_t(_dot(rxb, wa_ref[0]) + ba_ref[...])
    ig = _sigmoid_t(_dot(rxb, wx_ref[0]) + bx_ref[...])
    log_a = -R_C * r * _softplus(-lam_ref[...])
    a = jnp.exp(log_a)
    a_s[...] = a
    b_s[...] = jnp.sqrt(jnp.tanh(-log_a) * (a * a + 1.0)) * (ig * rx)

    def step(t, h):
        rows = pl.ds(t, B, stride=tl)
        h = a_s[rows, :] * h + b_s[rows, :]
        a_s[rows, :] = h
        return h

    h = lax.fori_loop(0, tl, step, hc_s[...], unroll=SUBLANE)
    hc_s[...] = h
    h1_ref[...] = h
    hr = a_s[...] * jax.nn.gelu(ry_ref[...].reshape(R, R_BDIM))
    hr_ref[...] = hr.reshape(B, tl, R_BDIM).astype(hr_ref.dtype)


def rglru(conv3, gates3, wa, wx, ba, bx, lam, h0, *, tl, out_dtype):
    B, L, _ = conv3.shape
    ry_blk = D_MODEL // R_BDIM
    vec = lambda n, j: (0, n)
    return pl.pallas_call(
        functools.partial(_rglru_body, B=B, tl=tl),
        grid=(R_BLOCKS, L // tl),
        in_specs=[pl.BlockSpec((B, tl, R_BDIM), lambda n, j: (0, j, n)),
                  pl.BlockSpec((B, tl, R_BDIM), lambda n, j: (0, j, ry_blk + n)),
                  pl.BlockSpec((1, R_BDIM, R_BDIM), lambda n, j: (n, 0, 0)),
                  pl.BlockSpec((1, R_BDIM, R_BDIM), lambda n, j: (n, 0, 0)),
                  pl.BlockSpec((1, R_BDIM), vec),
                  pl.BlockSpec((1, R_BDIM), vec),
                  pl.BlockSpec((1, R_BDIM), vec),
                  pl.BlockSpec((B, R_BDIM), vec)],
        out_specs=[pl.BlockSpec((B, tl, R_BDIM), lambda n, j: (0, j, n)),
                   pl.BlockSpec((B, R_BDIM), vec)],
        out_shape=[jax.ShapeDtypeStruct((B, L, R_WIDTH), out_dtype),
                   jax.ShapeDtypeStruct((B, R_WIDTH), F32)],
        scratch_shapes=[pltpu.VMEM((B * tl, R_BDIM), F32), pltpu.VMEM((B * tl, R_BDIM), F32),
                        pltpu.VMEM((B, R_BDIM), F32)],
        compiler_params=_cparams(2),
        name="rglru",
    )(conv3, gates3, wa, wx, ba, bx, lam, h0)


def _chunk_masks(c):
    t = lax.broadcasted_iota(jnp.int32, (c, c), 0)
    s = lax.broadcasted_iota(jnp.int32, (c, c), 1)
    return s <= t, s == t, s < t


def _col_from_row(row, eye):
    return jnp.sum(jnp.where(eye, row, 0.0), axis=1, keepdims=True)


def _row_from_col(col, eye):
    return jnp.sum(jnp.where(eye, col, 0.0), axis=0, keepdims=True)


def _split_iter(it, ncl):
    if ncl == 1:
        return it, 0
    return it // ncl, it % ncl


def _chunk_rows(ci, c):
    if isinstance(ci, int):
        return pl.ds(ci * c, c)
    return pl.ds(pl.multiple_of(ci * c, c), c)


def _mlstm_body(*refs, bt, bu, tl, c, aliased, carried):
    (q_ref, k_ref, v_ref, mo_ref, sm_ref, c0_ref, n0_ref, m0_ref, ib_ref, fb_ref, g_ref) = refs[:11]
    hm_ref, c1_ref, n1_ref, m1_ref = refs[11 + int(aliased):15 + int(aliased)]
    ncl = tl // c
    tri, eye, _ = _chunk_masks(c)
    scale = M_DH ** -0.5
    nch = bu * M_HEADS
    if carried:
        scr = refs[15 + int(aliased):]
        c_s, n_s, m_s = scr[:nch], scr[nch:2 * nch], scr[2 * nch:]
        j = pl.program_id(1)

        @pl.when(j == 0)
        def _():
            for bb in range(bu):
                for h in range(M_HEADS):
                    c_s[bb * M_HEADS + h][...] = c0_ref[bb, h]
                    n_s[bb * M_HEADS + h][...] = n0_ref[bb, h]
                    m_s[bb * M_HEADS + h][...] = m0_ref[bb, h]

    def chunk(it, carry):
        bg, ci = _split_iter(it, ncl)
        rows = _chunk_rows(ci, c)
        st = []
        for bb in range(bu):
            for h in range(M_HEADS):
                bi = bg * bu + bb
                d = dict(bi=bi, h=h, ch=bb * M_HEADS + h, cols=slice(h * M_DH, (h + 1) * M_DH))
                d["li_row"] = sm_ref[SM_MI + h, bi, pl.ds(ci, 1), :] + ib_ref[h]
                lf_row = -_softplus(-(sm_ref[SM_MF + h, bi, pl.ds(ci, 1), :] + fb_ref[h]))
                d["b_col"] = jnp.sum(jnp.where(tri, lf_row, 0.0), axis=1, keepdims=True)
                d["li_col"] = _col_from_row(d["li_row"], eye)
                st.append(d)
        for d in st:
            b_row = _row_from_col(d["b_col"], eye)
            d["logD"] = jnp.where(tri, d["b_col"] - b_row + d["li_row"], -jnp.inf)
            d["mx"] = jnp.max(d["logD"], axis=1, keepdims=True)
        for d in st:
            bi, h, ch, cols = d["bi"], d["h"], d["ch"], d["cols"]
            m = m_s[ch][:, 0:1] if carried else m0_ref[bi, h][:, 0:1]
            b_col = d["b_col"]
            inter = b_col + m
            mt = jnp.maximum(inter, d["mx"])
            m_new = mt[c - 1:c, :]
            b_last = b_col[c - 1:c, :]
            kf = k_ref[bi, rows, cols].astype(F32) * scale
            d.update(kf=kf, mt=mt, m_new=m_new, dec=jnp.exp(b_last + m - m_new),
                     kw=kf * jnp.exp(b_last - b_col + d["li_col"] - m_new),
                     wi=jnp.exp(inter - mt), dmat=jnp.exp(d["logD"] - mt))
        for d in st:
            bi, h, ch, cols = d["bi"], d["h"], d["ch"], d["cols"]
            qc = q_ref[bi, rows, cols].astype(BF16)
            vc = v_ref[bi, rows, cols].astype(BF16)
            C, n = (c_s[ch][...], n_s[ch][...]) if carried else (c0_ref[bi, h], n0_ref[bi, h])
            d["qk"] = _dot_nt(qc, d["kf"].astype(BF16))
            d["qC"] = _dot(qc, C.astype(BF16))
            d["qn"] = _dot_nt(qc, jnp.broadcast_to(n, (SUBLANE, M_DH)).astype(BF16))[:, 0:1]
            C_new = d["dec"] * C + _dot_tn(d["kw"].astype(BF16), vc)
            n_new = d["dec"] * n + jnp.sum(d["kw"], axis=0, keepdims=True)
            m_row = jnp.broadcast_to(d["m_new"], (1, LANE))
            if carried:
                c_s[ch][...], n_s[ch][...], m_s[ch][...] = C_new, n_new, m_row
            else:
                c1_ref[bi, h], n1_ref[bi, h], m1_ref[bi, h] = C_new, n_new, m_row
            d["vc"] = vc
        for d in st:
            s = d["qk"] * d["dmat"]
            d["sv"] = _dot(s.astype(BF16), d["vc"])
            d["rowsum"] = jnp.sum(s, axis=1, keepdims=True)
        for d in st:
            num = d["sv"] + d["wi"] * d["qC"]
            den = d["rowsum"] + d["wi"] * d["qn"]
            d["hh"] = num / jnp.maximum(jnp.abs(den), jnp.exp(-d["mt"]))
        for d in st:
            d["ms"] = jnp.mean(d["hh"] * d["hh"], axis=1, keepdims=True)
        for d in st:
            bi, cols = d["bi"], d["cols"]
            y = (d["hh"] * lax.rsqrt(d["ms"] + EPS) * g_ref[:, cols]
                 * jax.nn.sigmoid(mo_ref[bi, rows, cols]))
            hm_ref[bi, rows, cols] = y.astype(hm_ref.dtype)
        return carry

    lax.fori_loop(0, (bt // bu) * ncl, chunk, 0)

    if carried:
        @pl.when(j == pl.num_programs(1) - 1)
        def _():
            for bb in range(bu):
                for h in range(M_HEADS):
                    c1_ref[bb, h] = c_s[bb * M_HEADS + h][...]
                    n1_ref[bb, h] = n_s[bb * M_HEADS + h][...]
                    m1_ref[bb, h] = m_s[bb * M_HEADS + h][...]


def mlstm(mqkv3, gates3, small4, C0s, n0s, m0s, l_in, ib, fb, gain, C_prev, l, *, c, bt, bu, tl,
          out_dtype):
    B, L, _ = mqkv3.shape
    ncl = tl // c
    tok = lambda cb: pl.BlockSpec((bt, tl, D_MODEL), lambda i, j: (i, j, cb))
    st_in = lambda i, j: (l_in, i, 0, 0, 0)
    st_out = lambda i, j: (l, i, 0, 0, 0)
    st = lambda i, j: (i, 0, 0, 0)
    in_specs = [tok(0), tok(1), tok(2), tok(0),
                pl.BlockSpec((SUBLANE, bt, ncl, c), lambda i, j: (0, i, j, 0)),
                pl.BlockSpec((None, bt, M_HEADS, M_DH, M_DH), st_in),
                pl.BlockSpec((None, bt, M_HEADS, 1, M_DH), st_in),
                pl.BlockSpec((None, bt, M_HEADS, 1, LANE), st_in),
                _SMEM, _SMEM,
                pl.BlockSpec((1, D_MODEL), lambda i, j: (0, 0))]
    args = [mqkv3, mqkv3, mqkv3, gates3, small4, C0s, n0s, m0s, ib, fb, gain]
    aliases = {}
    if C_prev is not None:
        in_specs.append(_ANY)
        args.append(C_prev)
        aliases = {len(args) - 1: 1}
    carried = L > c
    scratch = []
    if carried:
        assert bt == bu
        nch = bu * M_HEADS
        scratch = ([pltpu.VMEM((M_DH, M_DH), F32)] * nch + [pltpu.VMEM((1, M_DH), F32)] * nch
                   + [pltpu.VMEM((1, LANE), F32)] * nch)
    return pl.pallas_call(
        functools.partial(_mlstm_body, bt=bt, bu=bu, tl=tl, c=c, aliased=C_prev is not None,
                          carried=carried),
        grid=(B // bt, L // tl),
        scratch_shapes=scratch,
        in_specs=in_specs,
        out_specs=[pl.BlockSpec((bt, tl, D_MODEL), lambda i, j: (i, j, 0)),
                   pl.BlockSpec((None, bt, M_HEADS, M_DH, M_DH), st_out),
                   pl.BlockSpec((bt, M_HEADS, 1, M_DH), st),
                   pl.BlockSpec((bt, M_HEADS, 1, LANE), st)],
        out_shape=[jax.ShapeDtypeStruct((B, L, D_MODEL), out_dtype),
                   jax.ShapeDtypeStruct((DEPTH, B, M_HEADS, M_DH, M_DH), F32),
                   jax.ShapeDtypeStruct((B, M_HEADS, 1, M_DH), F32),
                   jax.ShapeDtypeStruct((B, M_HEADS, 1, LANE), F32)],
        input_output_aliases=aliases,
        compiler_params=_cparams(2),
        name="mlstm",
    )(*args)


def _gdn_prep_body(q_ref, k_ref, v_ref, sb_ref, sa_ref, alog_ref, dtb_ref,
                   lm_ref, qk_ref, vb_ref, kbg_ref, qe_ref, kd_ref, eg_ref, *, bt, tl, c):
    ncl = tl // c
    tri, eye, strict = _chunk_masks(c)

    def chunk(it, carry):
        bi, ci = _split_iter(it, ncl)
        rows = _chunk_rows(ci, c)
        crow = pl.ds(ci, 1)
        for h in range(G_HEADS):
            cols = slice(h * G_DH, (h + 1) * G_DH)
            q = _l2norm(_silu(q_ref[bi, rows, cols])) * (G_DH ** -0.5)
            k = _l2norm(_silu(k_ref[bi, rows, cols]))
            v = _silu(v_ref[bi, rows, cols])
            beta_row = jax.nn.sigmoid(sb_ref[h, bi, crow, :])
            g_row = -jnp.exp(alog_ref[h]) * _softplus(sa_ref[h, bi, crow, :] + dtb_ref[h])
            G_col = jnp.sum(jnp.where(tri, g_row, 0.0), axis=1, keepdims=True)
            G_row = _row_from_col(G_col, eye)
            beta_col = _col_from_row(beta_row, eye)
            dec = jnp.exp(jnp.where(tri, G_col - G_row, -jnp.inf))
            kb16 = k.astype(BF16)
            kk = _dot_nt(kb16, kb16)
            lm_ref[h, bi, rows, :] = jnp.where(strict, beta_col * kk * dec, 0.0)
            qk_ref[h, bi, rows, :] = (_dot_nt(q.astype(BF16), kb16) * dec).astype(qk_ref.dtype)
            eG = jnp.exp(G_col)
            vb_ref[bi, rows, cols] = (v * beta_col).astype(vb_ref.dtype)
            kbg_ref[bi, rows, cols] = (k * beta_col * eG).astype(kbg_ref.dtype)
            qe_ref[bi, rows, cols] = (q * eG).astype(qe_ref.dtype)
            gl = G_col[c - 1:c, :]
            kd_ref[bi, rows, cols] = (k * jnp.exp(gl - G_col)).astype(kd_ref.dtype)
            eg_ref[h, bi, crow, :] = jnp.exp(G_row)
        return carry

    lax.fori_loop(0, bt * ncl, chunk, 0)


def gdn_prep(conv3, small4, alog, dtb, *, c, bt, tl, idt):
    B, L, _ = conv3.shape
    ncl = tl // c
    tok_in = lambda cb: pl.BlockSpec((bt, tl, D_MODEL), lambda i, j: (i, j, cb))
    tok = pl.BlockSpec((bt, tl, D_MODEL), lambda i, j: (i, j, 0))
    small = lambda blk: pl.BlockSpec((SUBLANE, bt, ncl, c), lambda i, j: (blk, i, j, 0))
    per_head = pl.BlockSpec((G_HEADS, bt, tl, c), lambda i, j: (0, i, j, 0))
    return pl.pallas_call(
        functools.partial(_gdn_prep_body, bt=bt, tl=tl, c=c),
        grid=(B // bt, L // tl),
        in_specs=[tok_in(1), tok_in(2), tok_in(3), small(SM_GB_BLK), small(SM_GA_BLK), _SMEM, _SMEM],
        out_specs=[per_head, per_head, tok, tok, tok, tok, small(0)],
        out_shape=[jax.ShapeDtypeStruct((G_HEADS, B, L, c), F32),
                   jax.ShapeDtypeStruct((G_HEADS, B, L, c), idt),
                   jax.ShapeDtypeStruct((B, L, D_MODEL), idt),
                   jax.ShapeDtypeStruct((B, L, D_MODEL), idt),
                   jax.ShapeDtypeStruct((B, L, D_MODEL), idt),
                   jax.ShapeDtypeStruct((B, L, D_MODEL), idt),
                   jax.ShapeDtypeStruct((G_HEADS, B, L // c, c), F32)],
        compiler_params=_cparams(2),
        name="gdn_prep",
    )(conv3, conv3, conv3, small4, small4, alog, dtb)


def _fsub_body(l_ref, x_ref, *, c):
    jrow = lax.broadcasted_iota(jnp.int32, (c, LANE), 0)
    x_ref[...] = jnp.zeros(x_ref.shape, F32)

    def t_body(t, carry):
        def sb_body(sb, acc):
            s0 = pl.multiple_of(sb * SUBLANE, SUBLANE)
            lblk = l_ref[t, pl.ds(s0, SUBLANE), :]
            for kk in range(SUBLANE):
                acc = acc - lblk[kk:kk + 1, :] * x_ref[s0 + kk]
            return acc

        acc0 = jnp.where(jrow == t, 1.0, 0.0)
        x_ref[t] = lax.fori_loop(0, (t + SUBLANE - 1) // SUBLANE, sb_body, acc0)
        return carry

    lax.fori_loop(0, c, t_body, 0)


def tri_inverse(lt):
    c, _, n = lt.shape
    blk = pl.BlockSpec((c, c, LANE), lambda i: (0, 0, i))
    return pl.pallas_call(
        functools.partial(_fsub_body, c=c),
        grid=(n // LANE,),
        in_specs=[blk], out_specs=blk,
        out_shape=jax.ShapeDtypeStruct((c, c, n), F32),
        compiler_params=_cparams(1),
        name="tri_inverse",
    )(lt)


def _gdn_rec_body(*refs, bt, tl, c, aliased, carried):
    (t_ref, qk_ref, vb_ref, kbg_ref, qe_ref, kd_ref, eg_ref, gz_ref, s0_ref, g_ref) = refs[:10]
    hd_ref, s1_ref = refs[10 + int(aliased):12 + int(aliased)]
    ncl = tl // c
    gain = g_ref[...]
    if carried:
        s_s = refs[12 + int(aliased):]
        j = pl.program_id(1)

        @pl.when(j == 0)
        def _():
            for h in range(G_HEADS):
                s_s[h][...] = s0_ref[0, h]

    def chunk(it, carry):
        bi, ci = _split_iter(it, ncl)
        rows = _chunk_rows(ci, c)
        heads = [slice(h * G_DH, (h + 1) * G_DH) for h in range(G_HEADS)]
        uw = []
        for h, cols in enumerate(heads):
            T = t_ref[h, bi, rows, :].astype(BF16)
            rhs = jnp.concatenate([vb_ref[bi, rows, cols], kbg_ref[bi, rows, cols]], axis=1)
            uw.append(_dot(T, rhs.astype(BF16)))
        ws = []
        for h, cols in enumerate(heads):
            S = s_s[h][...] if carried else s0_ref[bi, h]
            lhs = jnp.concatenate([uw[h][:, G_DH:].astype(qe_ref.dtype), qe_ref[bi, rows, cols]],
                                  axis=0)
            ws.append(_dot(lhs.astype(BF16), S.astype(BF16)))
        os_ = []
        for h, cols in enumerate(heads):
            u = (uw[h][:, :G_DH] - ws[h][:c]).astype(BF16)
            os_.append(ws[h][c:] + _dot(qk_ref[h, bi, rows, :].astype(BF16), u))
            dS = _dot_tn(kd_ref[bi, rows, cols].astype(BF16), u)
            egl = eg_ref[h, bi, pl.ds(ci, 1), :][:, c - 1:c]
            if carried:
                s_s[h][...] = egl * s_s[h][...] + dS
            else:
                s1_ref[bi, h] = egl * s0_ref[bi, h] + dS
        ms = [jnp.mean(o * o, axis=-1, keepdims=True) for o in os_]
        for h, cols in enumerate(heads):
            y = os_[h] * lax.rsqrt(ms[h] + EPS) * gain * _silu(gz_ref[bi, rows, cols])
            hd_ref[bi, rows, cols] = y.astype(hd_ref.dtype)
        return carry

    lax.fori_loop(0, bt * ncl, chunk, 0)

    if carried:
        @pl.when(j == pl.num_programs(1) - 1)
        def _():
            for h in range(G_HEADS):
                s1_ref[0, h] = s_s[h][...]


def gdn_rec(T, qk, vb, kbg, qe, kd, eg, gates3, S0s, l_in, gain, S_prev, l, *, c, bt, tl, out_dtype):
    B, L, _ = vb.shape
    ncl = tl // c
    tok = pl.BlockSpec((bt, tl, D_MODEL), lambda i, j: (i, j, 0))
    per_head = pl.BlockSpec((G_HEADS, bt, tl, c), lambda i, j: (0, i, j, 0))
    in_specs = [per_head, per_head, tok, tok, tok, tok,
                pl.BlockSpec((G_HEADS, bt, ncl, c), lambda i, j: (0, i, j, 0)),
                pl.BlockSpec((bt, tl, D_MODEL), lambda i, j: (i, j, 2)),
                pl.BlockSpec((None, bt, G_HEADS, G_DH, G_DH), lambda i, j: (l_in, i, 0, 0, 0)),
                pl.BlockSpec((1, G_DH), lambda i, j: (0, 0))]
    args = [T, qk, vb, kbg, qe, kd, eg, gates3, S0s, gain]
    aliases = {}
    if S_prev is not None:
        in_specs.append(_ANY)
        args.append(S_prev)
        aliases = {len(args) - 1: 1}
    carried = L > c
    if carried:
        assert bt == 1
    scratch = [pltpu.VMEM((G_DH, G_DH), F32)] * G_HEADS if carried else []
    return pl.pallas_call(
        functools.partial(_gdn_rec_body, bt=bt, tl=tl, c=c, aliased=S_prev is not None,
                          carried=carried),
        grid=(B // bt, L // tl),
        scratch_shapes=scratch,
        in_specs=in_specs,
        out_specs=[tok, pl.BlockSpec((None, bt, G_HEADS, G_DH, G_DH), lambda i, j: (l, i, 0, 0, 0))],
        out_shape=[jax.ShapeDtypeStruct((B, L, D_MODEL), out_dtype),
                   jax.ShapeDtypeStruct((DEPTH, B, G_HEADS, G_DH, G_DH), F32)],
        input_output_aliases=aliases,
        compiler_params=_cparams(2),
        name="gdn_rec",
    )(*args)


def _merge_body(x_ref, xn_ref, hm_ref, hr_ref, hd_ref, wg_ref, bg_ref, wb_ref, wo_ref,
                gx_ref, wq_ref, xo_ref, q_ref):
    xn = xn_ref[...]
    merged = None
    for n, br in enumerate((hm_ref, hr_ref, hd_ref)):
        lo = n * D_MODEL
        gate = jax.nn.sigmoid(_dot(xn, wg_ref[:, lo:lo + D_MODEL]) + bg_ref[:, lo:lo + D_MODEL])
        term = gate * _dot(br[...].astype(BF16), wb_ref[n])
        merged = term if merged is None else merged + term
    x = x_ref[...] + _dot(merged.astype(BF16), wo_ref[...])
    xo_ref[...] = x
    q_ref[...] = _dot(_rms(x, gx_ref[...]).astype(BF16), wq_ref[...]).astype(q_ref.dtype)


def merge(x, xn, hm, hr, hd, w_gate, b_gate, w_branch, w_out, xa_g, xa_wq, *, tm=512):
    M, D = x.shape
    tm = min(tm, M)
    row = pl.BlockSpec((tm, D), lambda i: (i, 0))
    f2 = lambda i: (0, 0)
    return pl.pallas_call(
        _merge_body,
        grid=(M // tm,),
        in_specs=[row, row, row, row, row,
                  _resident(w_gate.shape, f2), _resident(b_gate.shape, f2),
                  _resident(w_branch.shape, lambda i: (0, 0, 0)), _resident(w_out.shape, f2),
                  _resident(xa_g.shape, f2), _resident(xa_wq.shape, f2)],
        out_specs=[row, row],
        out_shape=[jax.ShapeDtypeStruct((M, D), F32), jax.ShapeDtypeStruct((M, D), F32)],
        compiler_params=_cparams(1),
        name="merge",
    )(x, xn, hm, hr, hd, w_gate, b_gate, w_branch, w_out, xa_g, xa_wq)


def _softmax_rows(s):
    e = jnp.exp(s - jnp.max(s, axis=-1, keepdims=True))
    return e / jnp.sum(e, axis=-1, keepdims=True)


def _xattn_body(q_ref, k_ref, v_ref, o_ref, *, bt, tq):
    scale = X_DH ** -0.5
    pairs = [(bi, h) for bi in range(bt) for h in range(X_HEADS)]
    hcol = lambda h: slice(h * X_DH, (h + 1) * X_DH)
    q = q_ref[...]
    scores = []
    for bi, h in pairs:
        qh = q[bi * tq:(bi + 1) * tq, hcol(h)].astype(BF16)
        scores.append(_dot_nt(qh, k_ref[bi, :, hcol(h)].astype(BF16)) * scale)
    outs = []
    for (bi, h), s in zip(pairs, scores):
        outs.append(_dot(_softmax_rows(s).astype(BF16), v_ref[bi, :, hcol(h)].astype(BF16)))
    rows = [jnp.concatenate(outs[bi * X_HEADS:(bi + 1) * X_HEADS], axis=1) for bi in range(bt)]
    o = rows[0] if bt == 1 else jnp.concatenate(rows, axis=0)
    o_ref[...] = o.astype(o_ref.dtype)


def _xattn_slot_head_body(q_ref, k_ref, v_ref, o_ref, *, bt, tq):
    scale = X_DH ** -0.5
    R = X_HEADS * tq
    NK = N_MEM * X_HEADS
    q_head = lax.broadcasted_iota(jnp.int32, (R, NK), 0) // tq
    k_head = lax.broadcasted_iota(jnp.int32, (R, NK), 1) % X_HEADS
    same = q_head == k_head
    q = q_ref[...]
    scores = []
    for bi in range(bt):
        qb = q[bi * tq:(bi + 1) * tq, :]
        q_rows = jnp.concatenate([qb[:, h * X_DH:(h + 1) * X_DH] for h in range(X_HEADS)], axis=0)
        k2 = k_ref[bi].reshape(NK, X_DH).astype(BF16)
        scores.append(jnp.where(same, _dot_nt(q_rows.astype(BF16), k2) * scale, -jnp.inf))
    rows = []
    for bi, s in enumerate(scores):
        v2 = v_ref[bi].reshape(NK, X_DH).astype(BF16)
        ob = _dot(_softmax_rows(s).astype(BF16), v2)
        rows.append(jnp.concatenate([ob[h * tq:(h + 1) * tq, :] for h in range(X_HEADS)], axis=1))
    o = rows[0] if bt == 1 else jnp.concatenate(rows, axis=0)
    o_ref[...] = o.astype(o_ref.dtype)


def xattn(q, mem_k, mem_v, lk, kcol, vcol, *, B, L, bt, tq):
    nq = L // tq
    R = bt * tq
    per_head = mem_k.ndim == 5
    if per_head:
        kv = lambda cb: pl.BlockSpec((None, bt, N_MEM, X_HEADS, X_DH),
                                     lambda i, j: (lk, i, 0, 0, 0))
    else:
        kv = lambda cb: pl.BlockSpec((None, bt, N_MEM, D_MODEL), lambda i, j: (lk, i, 0, cb))
    qo = pl.BlockSpec((R, D_MODEL), lambda i, j: (i * nq + j, 0))
    return pl.pallas_call(
        functools.partial(_xattn_slot_head_body if per_head else _xattn_body, bt=bt, tq=tq),
        grid=(B // bt, nq),
        in_specs=[qo, kv(kcol), kv(vcol)],
        out_specs=qo,
        out_shape=jax.ShapeDtypeStruct((B * L, D_MODEL), BF16),
        compiler_params=_cparams(2),
        name="xattn",
    )(q, mem_k, mem_v)


def _group_cfg(B, L):
    c = math.gcd(CHUNK, L)
    if L >= CHUNK:
        return dict(c=c, tl=min(512, L), m_bt=2, m_bu=2, g_bt=1, xa_bt=1, xa_tq=min(512, L),
                    idt=BF16)
    return dict(c=c, tl=L, m_bt=8, m_bu=2, g_bt=8, xa_bt=8, xa_tq=L, idt=F32)


def _layer(x, kv, state_in, l_in, prev, l, w, *, B, L, last_g):
    cfg = _group_cfg(B, L)
    c, tl, idt = cfg["c"], cfg["tl"], cfg["idt"]
    M = B * L
    conv0s, C0s, n0s, m0s, h0s, S0s = state_in
    C_prev, S_prev = prev if prev is not None else (None, None)

    x1, xn = ffn(x, w["f1_n"], w["f1_gu"], w["f1_d"], next_g=w["mix_n"], norm_dtype=BF16)

    conv2, conv_new = inproj_conv(xn, w["w_in_conv"], conv0s[l_in], w["conv_w"], w["conv_b"], B=B, L=L)
    mqkv = matmul(xn, w["w_in_mqkv"], idt, tm=1024, tn=1536)
    gates = matmul(xn, w["w_in_gates"], F32, tm=1024, tn=1536)
    small4 = matmul_t(w["w_in_small_t"], xn, tm=512).reshape(SMALL_PAD, B, L // c, c)
    conv3 = conv2.reshape(B, L, CONV_CH)
    mqkv3 = mqkv.reshape(B, L, 3 * D_MODEL)
    gates3 = gates.reshape(B, L, 3 * D_MODEL)

    hr, h1 = rglru(conv3, gates3, w["r_wa"], w["r_wx"], w["r_ba"], w["r_bx"], w["r_lam"],
                   h0s[l_in], tl=tl, out_dtype=idt)

    hm, C_out, n1, m1b = mlstm(mqkv3, gates3, small4, C0s, n0s, m0s, l_in, w["m_ib"], w["m_fb"],
                               w["m_norm"], C_prev, l, c=c, bt=cfg["m_bt"], bu=cfg["m_bu"], tl=tl,
                               out_dtype=idt)

    lm, qk, vb, kbg, qe, kd, eg = gdn_prep(conv3, small4, w["g_alog"], w["g_dtb"],
                                           c=c, bt=cfg["g_bt"], tl=tl, idt=idt)
    nchunks = M // c
    lt = lm.reshape(G_HEADS, nchunks, c, c).transpose(2, 3, 0, 1).reshape(c, c, G_HEADS * nchunks)
    tt = tri_inverse(lt)
    T = tt.reshape(c, c, G_HEADS, nchunks).transpose(2, 3, 0, 1).reshape(G_HEADS, B, L, c).astype(idt)
    hd, S_out = gdn_rec(T, qk, vb, kbg, qe, kd, eg, gates3, S0s, l_in, w["g_norm"], S_prev, l,
                        c=c, bt=cfg["g_bt"], tl=tl, out_dtype=idt)

    x2, q = merge(x1, xn, hm.reshape(M, D_MODEL), hr.reshape(M, D_MODEL), hd.reshape(M, D_MODEL),
                  w["w_gate"], w["b_gate"], w["w_branch"], w["w_out"], w["xa_n"], w["xa_wq"])
    mem_k, mem_v, lk, kcol, vcol = kv
    o = xattn(q, mem_k, mem_v, lk, kcol, vcol, B=B, L=L, bt=cfg["xa_bt"], tq=cfg["xa_tq"])
    if last_g is None:
        x3, y = ffn(x2, w["f2_n"], w["f2_gu"], w["f2_d"], proj=(o, w["xa_wo"]))
    else:
        x3, y = ffn(x2, w["f2_n"], w["f2_gu"], w["f2_d"], proj=(o, w["xa_wo"]),
                    next_g=last_g, norm_dtype=F32)
    small_state = (conv_new, n1.reshape(B, M_HEADS, M_DH), m1b[:, :, 0, 0], h1)
    return x3, y, small_state, (C_out, S_out)


def _prep_weights(l, ffn1_norm, ffn1_w_gu, ffn1_w_down, mix_norm, w_in, conv_w, conv_b, m_igate_b,
                  m_fgate_b, m_norm, r_wa, r_ba, r_wx, r_bx, r_lambda, g_a_log, g_dt_bias, g_norm,
                  w_gate, b_gate, w_branch, w_out, xa_norm, xa_wq, xa_wo, ffn2_norm, ffn2_w_gu,
                  ffn2_w_down):
    row = lambda a: a[l].reshape(1, -1).astype(F32)
    b16 = lambda a: a[l].astype(BF16)
    small_t = jnp.zeros((SMALL_PAD, D_MODEL), BF16).at[:N_SMALL].set(
        w_in[l][:, OFF_SMALL:OFF_SMALL + N_SMALL].T.astype(BF16))
    return dict(
        f1_n=row(ffn1_norm), f1_gu=b16(ffn1_w_gu), f1_d=b16(ffn1_w_down), mix_n=row(mix_norm),
        w_in_conv=w_in[l][:, :CONV_CH].astype(BF16),
        w_in_mqkv=w_in[l][:, OFF_MQKV:OFF_GATES].astype(BF16),
        w_in_gates=w_in[l][:, OFF_GATES:OFF_SMALL].astype(BF16),
        w_in_small_t=small_t,
        conv_w=conv_w[l].astype(F32), conv_b=row(conv_b),
        m_ib=m_igate_b[l].astype(F32), m_fb=m_fgate_b[l].astype(F32), m_norm=row(m_norm),
        r_wa=b16(r_wa), r_wx=b16(r_wx), r_ba=row(r_ba), r_bx=row(r_bx), r_lam=row(r_lambda),
        g_alog=g_a_log[l].astype(F32), g_dtb=g_dt_bias[l].astype(F32), g_norm=row(g_norm),
        w_gate=b16(w_gate), b_gate=row(b_gate), w_branch=b16(w_branch), w_out=b16(w_out),
        xa_n=row(xa_norm), xa_wq=b16(xa_wq), xa_wo=b16(xa_wo),
        f2_n=row(ffn2_norm), f2_gu=b16(ffn2_w_gu), f2_d=b16(ffn2_w_down))


def _stacked_state(conv, C, n, m, h, S):
    Dp, B = C.shape[:2]
    return (conv, C, n.reshape(Dp, B, M_HEADS, 1, M_DH),
            jnp.broadcast_to(m[..., None, None], (Dp, B, M_HEADS, 1, LANE)), h, S)


def kernel(x_prompt, x_sample, cache_mem_k, cache_mem_v, state_conv, state_mlstm_C, state_mlstm_n, state_mlstm_m, state_rglru_h, state_delta_S, mem_prompt, ffn1_norm, ffn1_w_gu, ffn1_w_down, mix_norm, w_in, conv_w, conv_b, m_igate_b, m_fgate_b, m_norm, r_wa, r_ba, r_wx, r_bx, r_lambda, g_a_log, g_dt_bias, g_norm, w_gate, b_gate, w_branch, w_out, xa_norm, xa_wq, xa_wk, xa_wv, xa_wo, ffn2_norm, ffn2_w_gu, ffn2_w_down, final_norm):
    Bp, Lp, D = x_prompt.shape
    Bs, Ls, _ = x_sample.shape
    stacked = (ffn1_norm, ffn1_w_gu, ffn1_w_down, mix_norm, w_in, conv_w, conv_b, m_igate_b,
               m_fgate_b, m_norm, r_wa, r_ba, r_wx, r_bx, r_lambda, g_a_log, g_dt_bias, g_norm,
               w_gate, b_gate, w_branch, w_out, xa_norm, xa_wq, xa_wo, ffn2_norm, ffn2_w_gu,
               ffn2_w_down)
    fin = final_norm.reshape(1, D).astype(F32)
    memp = mem_prompt.reshape(Bp * N_MEM, D).astype(BF16)
    p_state = _stacked_state(jnp.zeros((1, Bp, CONV_W - 1, CONV_CH), F32),
                             jnp.zeros((1, Bp, M_HEADS, M_DH, M_DH), F32),
                             jnp.zeros((1, Bp, M_HEADS, M_DH), F32),
                             jnp.full((1, Bp, M_HEADS), M_INIT, F32),
                             jnp.zeros((1, Bp, R_WIDTH), F32),
                             jnp.zeros((1, Bp, G_HEADS, G_DH, G_DH), F32))
    s_state = _stacked_state(state_conv, state_mlstm_C, state_mlstm_n, state_mlstm_m,
                             state_rglru_h, state_delta_S)

    yp = x_prompt.reshape(Bp * Lp, D)
    ys = x_sample.reshape(Bs * Ls, D)
    p_mk, p_mv, p_small, s_small = [], [], [], []
    p_big = s_big = None
    outp = outs = None
    for l in range(DEPTH):
        w = _prep_weights(l, *stacked)
        last_g = fin if l == DEPTH - 1 else None
        wkv = jnp.concatenate([xa_wk[l], xa_wv[l]], axis=1).astype(BF16)
        mkv = matmul(memp, wkv, F32, tm=512, tn=1024)
        p_mk.append(mkv[:, :D].reshape(Bp, N_MEM, X_HEADS, X_DH))
        p_mv.append(mkv[:, D:].reshape(Bp, N_MEM, X_HEADS, X_DH))
        kv_p = (mkv.reshape(1, Bp, N_MEM, 2 * D), mkv.reshape(1, Bp, N_MEM, 2 * D), 0, 0, 1)
        yp, outp, sm, p_big = _layer(yp, kv_p, p_state, 0, p_big, l, w, B=Bp, L=Lp, last_g=last_g)
        p_small.append(sm)
        ys, outs, sm, s_big = _layer(ys, (cache_mem_k, cache_mem_v, l, 0, 0), s_state, l, s_big, l, w,
                                     B=Bs, L=Ls, last_g=last_g)
        s_small.append(sm)

    def assemble(small, big):
        conv, n, m, h = [jnp.stack([sm[i] for sm in small]) for i in range(4)]
        C, S = big
        return conv, C, n, m, h, S

    y_prompt = outp.reshape(Bp, Lp, D)
    y_sample = outs.reshape(Bs, Ls, D)
    return (y_prompt, y_sample, jnp.stack(p_mk), jnp.stack(p_mv),
            *assemble(p_small, p_big), *assemble(s_small, s_big))
```

```python
import functools
import math

import jax
import jax.numpy as jnp
from jax import lax
from jax.experimental import pallas as pl
from jax.experimental.pallas import tpu as pltpu

F32 = jnp.float32
BF16 = jnp.bfloat16

D_MODEL = 1024
DEPTH = 4
N_MEM = 256
X_HEADS = 4
X_DH = D_MODEL // X_HEADS
M_HEADS = 4
M_DH = D_MODEL // M_HEADS
R_WIDTH = D_MODEL
R_BLOCKS = 8
R_BDIM = R_WIDTH // R_BLOCKS
R_C = 8.0
G_HEADS = 8
G_DH = D_MODEL // G_HEADS
N_BRANCH = 3
CONV_W = 4
CHUNK = 64
D_FF = 2816
EPS = 1e-6
M_INIT = -1e30

CONV_CH = R_WIDTH + 3 * D_MODEL
OFF_MQKV = CONV_CH
OFF_GATES = OFF_MQKV + 3 * D_MODEL
OFF_SMALL = OFF_GATES + 3 * D_MODEL
N_SMALL = 2 * M_HEADS + 2 * G_HEADS
SMALL_PAD = 32
SM_MI, SM_MF = 0, M_HEADS
SM_GB_BLK, SM_GA_BLK = 1, 2

CONV_SUB = 512
R_GROUP = 1
LANE = 128
SUBLANE = 8
VMEM_LIMIT = 52 * 1024 * 1024


def _cparams(n_axes, vmem=VMEM_LIMIT):
    return pltpu.CompilerParams(dimension_semantics=("arbitrary",) * n_axes,
                                vmem_limit_bytes=vmem)


def _resident(shape, index_map):
    return pl.BlockSpec(shape, index_map, pipeline_mode=pl.Buffered(1))


_SMEM = pl.BlockSpec(memory_space=pltpu.SMEM)
_ANY = pl.BlockSpec(memory_space=pl.ANY)


def _rms(x, g):
    return x * lax.rsqrt(jnp.mean(x * x, axis=-1, keepdims=True) + EPS) * g


def _softplus(x):
    return jnp.maximum(x, 0.0) + jnp.log1p(jnp.exp(-jnp.abs(x)))


def _sigmoid_t(x):
    return 0.5 * jnp.tanh(0.5 * x) + 0.5


def _l2norm(x):
    return x * lax.rsqrt(jnp.sum(x * x, axis=-1, keepdims=True) + EPS)


def _silu(x):
    return x * jax.nn.sigmoid(x)


def _dot(a, b):
    return jnp.dot(a, b, preferred_element_type=F32)


def _dot_nt(a, b):
    return lax.dot_general(a, b, (((1,), (1,)), ((), ())), preferred_element_type=F32)


def _dot_tn(a, b):
    return lax.dot_general(a, b, (((0,), (0,)), ((), ())), preferred_element_type=F32)


def _mm_body(x_ref, w_ref, o_ref):
    o_ref[...] = _dot(x_ref[...], w_ref[...]).astype(o_ref.dtype)


def matmul(x, w, out_dtype, tm, tn):
    M, K = x.shape
    N = w.shape[1]
    tm = min(tm, M)
    return pl.pallas_call(
        _mm_body,
        grid=(N // tn, M // tm),
        in_specs=[pl.BlockSpec((tm, K), lambda j, i: (i, 0)),
                  pl.BlockSpec((K, tn), lambda j, i: (0, j))],
        out_specs=pl.BlockSpec((tm, tn), lambda j, i: (i, j)),
        out_shape=jax.ShapeDtypeStruct((M, N), out_dtype),
        compiler_params=_cparams(2),
        name="matmul",
    )(x, w)


def _mm_t_body(w_ref, x_ref, o_ref):
    o_ref[...] = _dot_nt(w_ref[...], x_ref[...])


def matmul_t(wt, x, tm):
    P, K = wt.shape
    M = x.shape[0]
    tm = min(tm, M)
    return pl.pallas_call(
        _mm_t_body,
        grid=(M // tm,),
        in_specs=[pl.BlockSpec((P, K), lambda i: (0, 0)),
                  pl.BlockSpec((tm, K), lambda i: (i, 0))],
        out_specs=pl.BlockSpec((P, tm), lambda i: (0, i)),
        out_shape=jax.ShapeDtypeStruct((P, M), F32),
        compiler_params=_cparams(1),
        name="matmul_t",
    )(wt, x)


def _inproj_conv_body(x_ref, w_ref, c0_ref, cw_ref, cb_ref, o_ref, cn_ref, s_ref, *,
                      bt, Lt, seq_tiles):
    tn = w_ref.shape[1]
    pad = SUBLANE
    lo = pad - (CONV_W - 1)
    if seq_tiles == 1:
        s_ref[:, lo:pad, :] = c0_ref[...]
    else:
        first = (pl.program_id(1) % seq_tiles) == 0

        @pl.when(first)
        def _():
            s_ref[:, lo:pad, :] = c0_ref[...]

        @pl.when(jnp.logical_not(first))
        def _():
            s_ref[:, lo:pad, :] = s_ref[:, Lt + lo:Lt + pad, :]

    x = x_ref[...]
    for c0 in range(0, tn, CONV_SUB):
        cs = slice(c0, c0 + CONV_SUB)
        pc = _dot(x, w_ref[:, cs]).reshape(bt, Lt, CONV_SUB)
        s_ref[:, pad:pad + Lt, cs] = pc
        acc = cb_ref[:, cs][None] + pc * cw_ref[CONV_W - 1:CONV_W, cs][None]
        for j in range(CONV_W - 1):
            acc = acc + s_ref[:, lo + j:lo + j + Lt, cs] * cw_ref[j:j + 1, cs][None]
        o_ref[:, cs] = acc.reshape(bt * Lt, CONV_SUB)
    cn_ref[...] = s_ref[:, Lt + lo:Lt + pad, :]


def inproj_conv(xn, w, conv0, cw, cb, *, B, L, tm=512, tn=2048):
    M, K = xn.shape
    tm = min(tm, M)
    if L >= tm:
        bt, Lt, seq_tiles = 1, tm, L // tm
        seq = lambda j, i: (i // seq_tiles, 0, j)
    else:
        bt, Lt, seq_tiles = tm // L, L, 1
        seq = lambda j, i: (i, 0, j)
    return pl.pallas_call(
        functools.partial(_inproj_conv_body, bt=bt, Lt=Lt, seq_tiles=seq_tiles),
        grid=(CONV_CH // tn, M // tm),
        in_specs=[pl.BlockSpec((tm, K), lambda j, i: (i, 0)),
                  pl.BlockSpec((K, tn), lambda j, i: (0, j)),
                  pl.BlockSpec((bt, CONV_W - 1, tn), seq),
                  pl.BlockSpec((CONV_W, tn), lambda j, i: (0, j)),
                  pl.BlockSpec((1, tn), lambda j, i: (0, j))],
        out_specs=[pl.BlockSpec((tm, tn), lambda j, i: (i, j)),
                   pl.BlockSpec((bt, CONV_W - 1, tn), seq)],
        out_shape=[jax.ShapeDtypeStruct((M, CONV_CH), F32),
                   jax.ShapeDtypeStruct((B, CONV_W - 1, CONV_CH), F32)],
        scratch_shapes=[pltpu.VMEM((bt, Lt + SUBLANE, tn), F32)],
        compiler_params=_cparams(2),
        name="inproj_conv",
    )(xn, w, conv0, cw, cb)


def _ffn_body(*refs, has_proj, norm_dtype):
    it = iter(refs)
    x_ref = next(it)
    if has_proj:
        o_ref, wo_ref = next(it), next(it)
    g_ref, wgu_ref, wd_ref = next(it), next(it), next(it)
    if norm_dtype is not None:
        g2_ref = next(it)
    xo_ref = next(it)
    if norm_dtype is not None:
        no_ref = next(it)
    x = x_ref[...]
    if has_proj:
        x = x + _dot(o_ref[...], wo_ref[...])
    xn = _rms(x, g_ref[...]).astype(BF16)
    gu = _dot(xn, wgu_ref[...])
    g = gu[:, :D_FF]
    u = gu[:, D_FF:]
    a = (g * jax.nn.sigmoid(g) * u).astype(BF16)
    y = x + 0.5 * _dot(a, wd_ref[...])
    xo_ref[...] = y
    if norm_dtype is not None:
        no_ref[...] = _rms(y, g2_ref[...]).astype(norm_dtype)


def ffn(x, norm_g, w_gu, w_d, *, proj=None, next_g=None, norm_dtype=None, tm=512):
    M, D = x.shape
    tm = min(tm, M)
    row = lambda i: (i, 0)
    fixed = lambda i: (0, 0)
    args = [x]
    in_specs = [pl.BlockSpec((tm, D), row)]
    if proj is not None:
        o, wo = proj
        args += [o, wo]
        in_specs += [pl.BlockSpec((tm, D), row), _resident((D, D), fixed)]
    args += [norm_g, w_gu, w_d]
    in_specs += [_resident((1, D), fixed), _resident(w_gu.shape, fixed), _resident(w_d.shape, fixed)]
    out_shape = [jax.ShapeDtypeStruct((M, D), F32)]
    out_specs = [pl.BlockSpec((tm, D), row)]
    if norm_dtype is not None:
        args.append(next_g)
        in_specs.append(_resident((1, D), fixed))
        out_shape.append(jax.ShapeDtypeStruct((M, D), norm_dtype))
        out_specs.append(pl.BlockSpec((tm, D), row))
    res = pl.pallas_call(
        functools.partial(_ffn_body, has_proj=proj is not None, norm_dtype=norm_dtype),
        grid=(M // tm,),
        in_specs=in_specs, out_specs=out_specs, out_shape=out_shape,
        compiler_params=_cparams(1),
        name="ffn",
    )(*args)
    return res if norm_dtype is not None else (res[0], None)


def _rglru_body(rx_ref, ry_ref, wa_ref, wx_ref, ba_ref, bx_ref, lam_ref, h0_ref,
                hr_ref, h1_ref, hc_s, *ab_s, B, tl):
    R = B * tl
    W = R_GROUP * R_BDIM

    @pl.when(pl.program_id(1) == 0)
    def _():
        hc_s[...] = h0_ref[...]

    rx = rx_ref[...].reshape(R, W)
    rxb = rx.astype(BF16)
    blocks = [slice(g * R_BDIM, (g + 1) * R_BDIM) for g in range(R_GROUP)]
    pre_a = jnp.concatenate([_dot(rxb[:, cs], wa_ref[g]) for g, cs in enumerate(blocks)], axis=1)
    pre_x = jnp.concatenate([_dot(rxb[:, cs], wx_ref[g]) for g, cs in enumerate(blocks)], axis=1)
    r = _sigmoid_t(pre_a + ba_ref[...])
    ig = _sigmoid_t(pre_x + bx_ref[...])
    log_a = -R_C * r * _softplus(-lam_ref[...])
    a = jnp.exp(log_a)
    bx = jnp.sqrt(jnp.tanh(-log_a) * (a * a + 1.0)) * (ig * rx)
    a_s, b_s = ab_s[:R_GROUP], ab_s[R_GROUP:]
    for g, cs in enumerate(blocks):
        a_s[g][...] = a[:, cs]
        b_s[g][...] = bx[:, cs]

    def step(t, hs):
        rows = pl.ds(t, B, stride=tl)
        new = []
        for g in range(R_GROUP):
            h = a_s[g][rows, :] * hs[g] + b_s[g][rows, :]
            a_s[g][rows, :] = h
            new.append(h)
        return tuple(new)

    hs = lax.fori_loop(0, tl, step, tuple(hc_s[:, cs] for cs in blocks), unroll=SUBLANE)
    h = jnp.concatenate(hs, axis=1)
    hc_s[...] = h
    h1_ref[...] = h
    hall = jnp.concatenate([a_s[g][...] for g in range(R_GROUP)], axis=1)
    hr = hall * jax.nn.gelu(ry_ref[...].reshape(R, W))
    hr_ref[...] = hr.reshape(B, tl, W).astype(hr_ref.dtype)


def rglru(conv3, gates3, wa, wx, ba, bx, lam, h0, *, tl, out_dtype):
    B, L, _ = conv3.shape
    W = R_GROUP * R_BDIM
    ry_blk = D_MODEL // W
    vec = lambda n, j: (0, n)
    return pl.pallas_call(
        functools.partial(_rglru_body, B=B, tl=tl),
        grid=(R_BLOCKS // R_GROUP, L // tl),
        in_specs=[pl.BlockSpec((B, tl, W), lambda n, j: (0, j, n)),
                  pl.BlockSpec((B, tl, W), lambda n, j: (0, j, ry_blk + n)),
                  pl.BlockSpec((R_GROUP, R_BDIM, R_BDIM), lambda n, j: (n, 0, 0)),
                  pl.BlockSpec((R_GROUP, R_BDIM, R_BDIM), lambda n, j: (n, 0, 0)),
                  pl.BlockSpec((1, W), vec),
                  pl.BlockSpec((1, W), vec),
                  pl.BlockSpec((1, W), vec),
                  pl.BlockSpec((B, W), vec)],
        out_specs=[pl.BlockSpec((B, tl, W), lambda n, j: (0, j, n)),
                   pl.BlockSpec((B, W), vec)],
        out_shape=[jax.ShapeDtypeStruct((B, L, R_WIDTH), out_dtype),
                   jax.ShapeDtypeStruct((B, R_WIDTH), F32)],
        scratch_shapes=[pltpu.VMEM((B, W), F32)]
        + [pltpu.VMEM((B * tl, R_BDIM), F32)] * (2 * R_GROUP),
        compiler_params=_cparams(2),
        name="rglru",
    )(conv3, gates3, wa, wx, ba, bx, lam, h0)


def _chunk_masks(c):
    t = lax.broadcasted_iota(jnp.int32, (c, c), 0)
    s = lax.broadcasted_iota(jnp.int32, (c, c), 1)
    return s <= t, s == t, s < t


def _col_from_row(row, eye):
    return jnp.sum(jnp.where(eye, row, 0.0), axis=1, keepdims=True)


def _row_from_col(col, eye):
    return jnp.sum(jnp.where(eye, col, 0.0), axis=0, keepdims=True)


def _split_iter(it, ncl):
    if ncl == 1:
        return it, 0
    return it // ncl, it % ncl


def _chunk_rows(ci, c):
    if isinstance(ci, int):
        return pl.ds(ci * c, c)
    return pl.ds(pl.multiple_of(ci * c, c), c)


def _mlstm_body(*refs, bt, bu, tl, c, aliased, carried):
    (q_ref, k_ref, v_ref, mo_ref, sm_ref, c0_ref, n0_ref, m0_ref, ib_ref, fb_ref, g_ref) = refs[:11]
    hm_ref, c1_ref, n1_ref, m1_ref = refs[11 + int(aliased):15 + int(aliased)]
    ncl = tl // c
    tri, eye, _ = _chunk_masks(c)
    scale = M_DH ** -0.5
    nch = bu * M_HEADS
    if carried:
        scr = refs[15 + int(aliased):]
        c_s, n_s, m_s = scr[:nch], scr[nch:2 * nch], scr[2 * nch:]
        j = pl.program_id(1)

        @pl.when(j == 0)
        def _():
            for bb in range(bu):
                for h in range(M_HEADS):
                    c_s[bb * M_HEADS + h][...] = c0_ref[bb, h]
                    n_s[bb * M_HEADS + h][...] = n0_ref[bb, h]
                    m_s[bb * M_HEADS + h][...] = m0_ref[bb, h]

    def chunk(it, carry):
        bg, ci = _split_iter(it, ncl)
        rows = _chunk_rows(ci, c)
        st = []
        for bb in range(bu):
            for h in range(M_HEADS):
                bi = bg * bu + bb
                d = dict(bi=bi, h=h, ch=bb * M_HEADS + h, cols=slice(h * M_DH, (h + 1) * M_DH))
                d["li_row"] = sm_ref[SM_MI + h, bi, pl.ds(ci, 1), :] + ib_ref[h]
                lf_row = -_softplus(-(sm_ref[SM_MF + h, bi, pl.ds(ci, 1), :] + fb_ref[h]))
                d["b_col"] = jnp.sum(jnp.where(tri, lf_row, 0.0), axis=1, keepdims=True)
                d["li_col"] = _col_from_row(d["li_row"], eye)
                st.append(d)
        for d in st:
            b_row = _row_from_col(d["b_col"], eye)
            d["logD"] = jnp.where(tri, d["b_col"] - b_row + d["li_row"], -jnp.inf)
            d["mx"] = jnp.max(d["logD"], axis=1, keepdims=True)
        for d in st:
            bi, h, ch, cols = d["bi"], d["h"], d["ch"], d["cols"]
            m = m_s[ch][:, 0:1] if carried else m0_ref[bi, h][:, 0:1]
            b_col = d["b_col"]
            inter = b_col + m
            mt = jnp.maximum(inter, d["mx"])
            m_new = mt[c - 1:c, :]
            b_last = b_col[c - 1:c, :]
            kf = k_ref[bi, rows, cols].astype(F32) * scale
            d.update(kf=kf, mt=mt, m_new=m_new, dec=jnp.exp(b_last + m - m_new),
                     kw=kf * jnp.exp(b_last - b_col + d["li_col"] - m_new),
                     wi=jnp.exp(inter - mt), dmat=jnp.exp(d["logD"] - mt))
        for d in st:
            bi, h, ch, cols = d["bi"], d["h"], d["ch"], d["cols"]
            qc = q_ref[bi, rows, cols].astype(BF16)
            vc = v_ref[bi, rows, cols].astype(BF16)
            C, n = (c_s[ch][...], n_s[ch][...]) if carried else (c0_ref[bi, h], n0_ref[bi, h])
            d["qk"] = _dot_nt(qc, d["kf"].astype(BF16))
            d["qC"] = _dot(qc, C.astype(BF16))
            d["qn"] = _dot_nt(qc, jnp.broadcast_to(n, (SUBLANE, M_DH)).astype(BF16))[:, 0:1]
            C_new = d["dec"] * C + _dot_tn(d["kw"].astype(BF16), vc)
            n_new = d["dec"] * n + jnp.sum(d["kw"], axis=0, keepdims=True)
            m_row = jnp.broadcast_to(d["m_new"], (1, LANE))
            if carried:
                c_s[ch][...], n_s[ch][...], m_s[ch][...] = C_new, n_new, m_row
            else:
                c1_ref[bi, h], n1_ref[bi, h], m1_ref[bi, h] = C_new, n_new, m_row
            d["vc"] = vc
        for d in st:
            s = d["qk"] * d["dmat"]
            d["sv"] = _dot(s.astype(BF16), d["vc"])
            d["rowsum"] = jnp.sum(s, axis=1, keepdims=True)
        for d in st:
            num = d["sv"] + d["wi"] * d["qC"]
            den = d["rowsum"] + d["wi"] * d["qn"]
            d["hh"] = num / jnp.maximum(jnp.abs(den), jnp.exp(-d["mt"]))
        for d in st:
            d["ms"] = jnp.mean(d["hh"] * d["hh"], axis=1, keepdims=True)
        for d in st:
            bi, cols = d["bi"], d["cols"]
            y = (d["hh"] * lax.rsqrt(d["ms"] + EPS) * g_ref[:, cols]
                 * jax.nn.sigmoid(mo_ref[bi, rows, cols]))
            hm_ref[bi, rows, cols] = y.astype(hm_ref.dtype)
        return carry

    lax.fori_loop(0, (bt // bu) * ncl, chunk, 0)

    if carried:
        @pl.when(j == pl.num_programs(1) - 1)
        def _():
            for bb in range(bu):
                for h in range(M_HEADS):
                    c1_ref[bb, h] = c_s[bb * M_HEADS + h][...]
                    n1_ref[bb, h] = n_s[bb * M_HEADS + h][...]
                    m1_ref[bb, h] = m_s[bb * M_HEADS + h][...]


def mlstm(mqkv3, gates3, small4, C0s, n0s, m0s, l_in, ib, fb, gain, C_prev, l, *, c, bt, bu, tl,
          out_dtype):
    B, L, _ = mqkv3.shape
    ncl = tl // c
    tok = lambda cb: pl.BlockSpec((bt, tl, D_MODEL), lambda i, j: (i, j, cb))
    st_in = lambda i, j: (l_in, i, 0, 0, 0)
    st_out = lambda i, j: (l, i, 0, 0, 0)
    st = lambda i, j: (i, 0, 0, 0)
    in_specs = [tok(0), tok(1), tok(2), tok(0),
                pl.BlockSpec((SUBLANE, bt, ncl, c), lambda i, j: (0, i, j, 0)),
                pl.BlockSpec((None, bt, M_HEADS, M_DH, M_DH), st_in),
                pl.BlockSpec((None, bt, M_HEADS, 1, M_DH), st_in),
                pl.BlockSpec((None, bt, M_HEADS, 1, LANE), st_in),
                _SMEM, _SMEM,
                pl.BlockSpec((1, D_MODEL), lambda i, j: (0, 0))]
    args = [mqkv3, mqkv3, mqkv3, gates3, small4, C0s, n0s, m0s, ib, fb, gain]
    aliases = {}
    if C_prev is not None:
        in_specs.append(_ANY)
        args.append(C_prev)
        aliases = {len(args) - 1: 1}
    carried = L > c
    scratch = []
    if carried:
        assert bt == bu
        nch = bu * M_HEADS
        scratch = ([pltpu.VMEM((M_DH, M_DH), F32)] * nch + [pltpu.VMEM((1, M_DH), F32)] * nch
                   + [pltpu.VMEM((1, LANE), F32)] * nch)
    return pl.pallas_call(
        functools.partial(_mlstm_body, bt=bt, bu=bu, tl=tl, c=c, aliased=C_prev is not None,
                          carried=carried),
        grid=(B // bt, L // tl),
        scratch_shapes=scratch,
        in_specs=in_specs,
        out_specs=[pl.BlockSpec((bt, tl, D_MODEL), lambda i, j: (i, j, 0)),
                   pl.BlockSpec((None, bt, M_HEADS, M_DH, M_DH), st_out),
                   pl.BlockSpec((bt, M_HEADS, 1, M_DH), st),
                   pl.BlockSpec((bt, M_HEADS, 1, LANE), st)],
        out_shape=[jax.ShapeDtypeStruct((B, L, D_MODEL), out_dtype),
                   jax.ShapeDtypeStruct((DEPTH, B, M_HEADS, M_DH, M_DH), F32),
                   jax.ShapeDtypeStruct((B, M_HEADS, 1, M_DH), F32),
                   jax.ShapeDtypeStruct((B, M_HEADS, 1, LANE), F32)],
        input_output_aliases=aliases,
        compiler_params=_cparams(2),
        name="mlstm",
    )(*args)


def _gdn_prep_body(q_ref, k_ref, v_ref, sb_ref, sa_ref, alog_ref, dtb_ref,
                   lm_ref, qk_ref, vb_ref, kbg_ref, qe_ref, kd_ref, eg_ref, *, bt, tl, c):
    ncl = tl // c
    tri, eye, strict = _chunk_masks(c)

    def chunk(it, carry):
        bi, ci = _split_iter(it, ncl)
        rows = _chunk_rows(ci, c)
        crow = pl.ds(ci, 1)
        for h in range(G_HEADS):
            cols = slice(h * G_DH, (h + 1) * G_DH)
            q = _l2norm(_silu(q_ref[bi, rows, cols])) * (G_DH ** -0.5)
            k = _l2norm(_silu(k_ref[bi, rows, cols]))
            v = _silu(v_ref[bi, rows, cols])
            beta_row = jax.nn.sigmoid(sb_ref[h, bi, crow, :])
            g_row = -jnp.exp(alog_ref[h]) * _softplus(sa_ref[h, bi, crow, :] + dtb_ref[h])
            G_col = jnp.sum(jnp.where(tri, g_row, 0.0), axis=1, keepdims=True)
            G_row = _row_from_col(G_col, eye)
            beta_col = _col_from_row(beta_row, eye)
            dec = jnp.exp(jnp.where(tri, G_col - G_row, -jnp.inf))
            kb16 = k.astype(BF16)
            kk = _dot_nt(kb16, kb16)
            lm_ref[h, bi, rows, :] = jnp.where(strict, beta_col * kk * dec, 0.0)
            qk_ref[h, bi, rows, :] = (_dot_nt(q.astype(BF16), kb16) * dec).astype(qk_ref.dtype)
            eG = jnp.exp(G_col)
            vb_ref[bi, rows, cols] = (v * beta_col).astype(vb_ref.dtype)
            kbg_ref[bi, rows, cols] = (k * beta_col * eG).astype(kbg_ref.dtype)
            qe_ref[bi, rows, cols] = (q * eG).astype(qe_ref.dtype)
            gl = G_col[c - 1:c, :]
            kd_ref[bi, rows, cols] = (k * jnp.exp(gl - G_col)).astype(kd_ref.dtype)
            eg_ref[h, bi, crow, :] = jnp.exp(G_row)
        return carry

    lax.fori_loop(0, bt * ncl, chunk, 0)


def gdn_prep(conv3, small4, alog, dtb, *, c, bt, tl, idt):
    B, L, _ = conv3.shape
    ncl = tl // c
    tok_in = lambda cb: pl.BlockSpec((bt, tl, D_MODEL), lambda i, j: (i, j, cb))
    tok = pl.BlockSpec((bt, tl, D_MODEL), lambda i, j: (i, j, 0))
    small = lambda blk: pl.BlockSpec((SUBLANE, bt, ncl, c), lambda i, j: (blk, i, j, 0))
    per_head = pl.BlockSpec((G_HEADS, bt, tl, c), lambda i, j: (0, i, j, 0))
    return pl.pallas_call(
        functools.partial(_gdn_prep_body, bt=bt, tl=tl, c=c),
        grid=(B // bt, L // tl),
        in_specs=[tok_in(1), tok_in(2), tok_in(3), small(SM_GB_BLK), small(SM_GA_BLK), _SMEM, _SMEM],
        out_specs=[per_head, per_head, tok, tok, tok, tok, small(0)],
        out_shape=[jax.ShapeDtypeStruct((G_HEADS, B, L, c), F32),
                   jax.ShapeDtypeStruct((G_HEADS, B, L, c), idt),
                   jax.ShapeDtypeStruct((B, L, D_MODEL), idt),
                   jax.ShapeDtypeStruct((B, L, D_MODEL), idt),
                   jax.ShapeDtypeStruct((B, L, D_MODEL), idt),
                   jax.ShapeDtypeStruct((B, L, D_MODEL), idt),
                   jax.ShapeDtypeStruct((G_HEADS, B, L // c, c), F32)],
        compiler_params=_cparams(2),
        name="gdn_prep",
    )(conv3, conv3, conv3, small4, small4, alog, dtb)


def _fsub_body(l_ref, x_ref, *, c):
    jrow = lax.broadcasted_iota(jnp.int32, (SUBLANE, LANE), 0)
    x_ref[...] = jnp.zeros(x_ref.shape, F32)
    grp = lambda g: slice(g * SUBLANE, (g + 1) * SUBLANE)

    for tb in range(c // SUBLANE):
        def r_body(r, carry, tb=tb):
            t = tb * SUBLANE + r
            acc = [jnp.zeros((SUBLANE, LANE), F32) for _ in range(tb)]
            acc.append(jnp.where(jrow == r, 1.0, 0.0))
            for sb in range(tb + 1):
                lblk = l_ref[t, grp(sb), :]
                for kk in range(SUBLANE):
                    lrow = lblk[kk:kk + 1, :]
                    for jg in range(sb + 1):
                        acc[jg] = acc[jg] - lrow * x_ref[sb * SUBLANE + kk, grp(jg), :]
            for jg in range(tb + 1):
                x_ref[t, grp(jg), :] = acc[jg]
            return carry

        lax.fori_loop(0, SUBLANE, r_body, 0)


def tri_inverse(lt):
    c, _, n = lt.shape
    blk = pl.BlockSpec((c, c, LANE), lambda i: (0, 0, i))
    return pl.pallas_call(
        functools.partial(_fsub_body, c=c),
        grid=(n // LANE,),
        in_specs=[blk], out_specs=blk,
        out_shape=jax.ShapeDtypeStruct((c, c, n), F32),
        compiler_params=_cparams(1),
        name="tri_inverse",
    )(lt)


def _gdn_rec_body(*refs, bt, bu, tl, c, aliased, carried):
    (t_ref, qk_ref, vb_ref, kbg_ref, qe_ref, kd_ref, eg_ref, gz_ref, s0_ref, g_ref) = refs[:10]
    hd_ref, s1_ref = refs[10 + int(aliased):12 + int(aliased)]
    ncl = tl // c
    gain = g_ref[...]
    chains = [(bb, h) for bb in range(bu) for h in range(G_HEADS)]
    if carried:
        s_s = refs[12 + int(aliased):]
        j = pl.program_id(1)

        @pl.when(j == 0)
        def _():
            for k, (bb, h) in enumerate(chains):
                s_s[k][...] = s0_ref[bb, h]

    def chunk(it, carry):
        bg, ci = _split_iter(it, ncl)
        rows = _chunk_rows(ci, c)
        ch = [(bg * bu + bb, h, slice(h * G_DH, (h + 1) * G_DH)) for bb, h in chains]
        uw = []
        for bi, h, cols in ch:
            T = t_ref[h, bi, rows, :].astype(BF16)
            rhs = jnp.concatenate([vb_ref[bi, rows, cols], kbg_ref[bi, rows, cols]], axis=1)
            uw.append(_dot(T, rhs.astype(BF16)))
        ws = []
        for k, (bi, h, cols) in enumerate(ch):
            S = s_s[k][...] if carried else s0_ref[bi, h]
            lhs = jnp.concatenate([uw[k][:, G_DH:].astype(qe_ref.dtype), qe_ref[bi, rows, cols]],
                                  axis=0)
            ws.append(_dot(lhs.astype(BF16), S.astype(BF16)))
        os_ = []
        for k, (bi, h, cols) in enumerate(ch):
            u = (uw[k][:, :G_DH] - ws[k][:c]).astype(BF16)
            os_.append(ws[k][c:] + _dot(qk_ref[h, bi, rows, :].astype(BF16), u))
            dS = _dot_tn(kd_ref[bi, rows, cols].astype(BF16), u)
            egl = eg_ref[h, bi, pl.ds(ci, 1), :][:, c - 1:c]
            if carried:
                s_s[k][...] = egl * s_s[k][...] + dS
            else:
                s1_ref[bi, h] = egl * s0_ref[bi, h] + dS
        ms = [jnp.mean(o * o, axis=-1, keepdims=True) for o in os_]
        for k, (bi, h, cols) in enumerate(ch):
            y = os_[k] * lax.rsqrt(ms[k] + EPS) * gain * _silu(gz_ref[bi, rows, cols])
            hd_ref[bi, rows, cols] = y.astype(hd_ref.dtype)
        return carry

    lax.fori_loop(0, (bt // bu) * ncl, chunk, 0)

    if carried:
        @pl.when(j == pl.num_programs(1) - 1)
        def _():
            for k, (bb, h) in enumerate(chains):
                s1_ref[bb, h] = s_s[k][...]


def gdn_rec(T, qk, vb, kbg, qe, kd, eg, gates3, S0s, l_in, gain, S_prev, l, *, c, bt, bu, tl,
            out_dtype):
    B, L, _ = vb.shape
    ncl = tl // c
    tok = pl.BlockSpec((bt, tl, D_MODEL), lambda i, j: (i, j, 0))
    per_head = pl.BlockSpec((G_HEADS, bt, tl, c), lambda i, j: (0, i, j, 0))
    in_specs = [per_head, per_head, tok, tok, tok, tok,
                pl.BlockSpec((G_HEADS, bt, ncl, c), lambda i, j: (0, i, j, 0)),
                pl.BlockSpec((bt, tl, D_MODEL), lambda i, j: (i, j, 2)),
                pl.BlockSpec((None, bt, G_HEADS, G_DH, G_DH), lambda i, j: (l_in, i, 0, 0, 0)),
                pl.BlockSpec((1, G_DH), lambda i, j: (0, 0))]
    args = [T, qk, vb, kbg, qe, kd, eg, gates3, S0s, gain]
    aliases = {}
    if S_prev is not None:
        in_specs.append(_ANY)
        args.append(S_prev)
        aliases = {len(args) - 1: 1}
    carried = L > c
    if carried:
        assert bt == bu
    scratch = [pltpu.VMEM((G_DH, G_DH), F32)] * (bu * G_HEADS) if carried else []
    return pl.pallas_call(
        functools.partial(_gdn_rec_body, bt=bt, bu=bu, tl=tl, c=c, aliased=S_prev is not None,
                          carried=carried),
        grid=(B // bt, L // tl),
        scratch_shapes=scratch,
        in_specs=in_specs,
        out_specs=[tok, pl.BlockSpec((None, bt, G_HEADS, G_DH, G_DH), lambda i, j: (l, i, 0, 0, 0))],
        out_shape=[jax.ShapeDtypeStruct((B, L, D_MODEL), out_dtype),
                   jax.ShapeDtypeStruct((DEPTH, B, G_HEADS, G_DH, G_DH), F32)],
        input_output_aliases=aliases,
        compiler_params=_cparams(2),
        name="gdn_rec",
    )(*args)


def _merge_body(x_ref, xn_ref, hm_ref, hr_ref, hd_ref, wg_ref, bg_ref, wb_ref, wo_ref,
                gx_ref, wq_ref, xo_ref, q_ref):
    xn = xn_ref[...]
    merged = None
    for n, br in enumerate((hm_ref, hr_ref, hd_ref)):
        lo = n * D_MODEL
        gate = jax.nn.sigmoid(_dot(xn, wg_ref[:, lo:lo + D_MODEL]) + bg_ref[:, lo:lo + D_MODEL])
        term = gate * _dot(br[...].astype(BF16), wb_ref[n])
        merged = term if merged is None else merged + term
    x = x_ref[...] + _dot(merged.astype(BF16), wo_ref[...])
    xo_ref[...] = x
    q_ref[...] = _dot(_rms(x, gx_ref[...]).astype(BF16), wq_ref[...]).astype(q_ref.dtype)


def merge(x, xn, hm, hr, hd, w_gate, b_gate, w_branch, w_out, xa_g, xa_wq, *, tm=512):
    M, D = x.shape
    tm = min(tm, M)
    row = pl.BlockSpec((tm, D), lambda i: (i, 0))
    f2 = lambda i: (0, 0)
    return pl.pallas_call(
        _merge_body,
        grid=(M // tm,),
        in_specs=[row, row, row, row, row,
                  _resident(w_gate.shape, f2), _resident(b_gate.shape, f2),
                  _resident(w_branch.shape, lambda i: (0, 0, 0)), _resident(w_out.shape, f2),
                  _resident(xa_g.shape, f2), _resident(xa_wq.shape, f2)],
        out_specs=[row, row],
        out_shape=[jax.ShapeDtypeStruct((M, D), F32), jax.ShapeDtypeStruct((M, D), F32)],
        compiler_params=_cparams(1),
        name="merge",
    )(x, xn, hm, hr, hd, w_gate, b_gate, w_branch, w_out, xa_g, xa_wq)


def _softmax_rows(s):
    e = jnp.exp(s - jnp.max(s, axis=-1, keepdims=True))
    return e / jnp.sum(e, axis=-1, keepdims=True)


def _xattn_body(q_ref, k_ref, v_ref, o_ref, *, bt, tq):
    scale = X_DH ** -0.5
    pairs = [(bi, h) for bi in range(bt) for h in range(X_HEADS)]
    hcol = lambda h: slice(h * X_DH, (h + 1) * X_DH)
    q = q_ref[...]
    scores = []
    for bi, h in pairs:
        qh = q[bi * tq:(bi + 1) * tq, hcol(h)].astype(BF16)
        scores.append(_dot_nt(qh, k_ref[bi, :, hcol(h)].astype(BF16)) * scale)
    outs = []
    for (bi, h), s in zip(pairs, scores):
        outs.append(_dot(_softmax_rows(s).astype(BF16), v_ref[bi, :, hcol(h)].astype(BF16)))
    rows = [jnp.concatenate(outs[bi * X_HEADS:(bi + 1) * X_HEADS], axis=1) for bi in range(bt)]
    o = rows[0] if bt == 1 else jnp.concatenate(rows, axis=0)
    o_ref[...] = o.astype(o_ref.dtype)


def _xattn_slot_head_body(q_ref, k_ref, v_ref, o_ref, *, bt, tq):
    scale = X_DH ** -0.5
    R = X_HEADS * tq
    NK = N_MEM * X_HEADS
    q_head = lax.broadcasted_iota(jnp.int32, (R, NK), 0) // tq
    k_head = lax.broadcasted_iota(jnp.int32, (R, NK), 1) % X_HEADS
    same = q_head == k_head
    q = q_ref[...]
    scores = []
    for bi in range(bt):
        qb = q[bi * tq:(bi + 1) * tq, :]
        q_rows = jnp.concatenate([qb[:, h * X_DH:(h + 1) * X_DH] for h in range(X_HEADS)], axis=0)
        k2 = k_ref[bi].reshape(NK, X_DH).astype(BF16)
        scores.append(jnp.where(same, _dot_nt(q_rows.astype(BF16), k2) * scale, -jnp.inf))
    rows = []
    for bi, s in enumerate(scores):
        v2 = v_ref[bi].reshape(NK, X_DH).astype(BF16)
        ob = _dot(_softmax_rows(s).astype(BF16), v2)
        rows.append(jnp.concatenate([ob[h * tq:(h + 1) * tq, :] for h in range(X_HEADS)], axis=1))
    o = rows[0] if bt == 1 else jnp.concatenate(rows, axis=0)
    o_ref[...] = o.astype(o_ref.dtype)


def xattn(q, mem_k, mem_v, lk, kcol, vcol, *, B, L, bt, tq):
    nq = L // tq
    R = bt * tq
    per_head = mem_k.ndim == 5
    if per_head:
        kv = lambda cb: pl.BlockSpec((None, bt, N_MEM, X_HEADS, X_DH),
                                     lambda i, j: (lk, i, 0, 0, 0))
    else:
        kv = lambda cb: pl.BlockSpec((None, bt, N_MEM, D_MODEL), lambda i, j: (lk, i, 0, cb))
    qo = pl.BlockSpec((R, D_MODEL), lambda i, j: (i * nq + j, 0))
    return pl.pallas_call(
        functools.partial(_xattn_slot_head_body if per_head else _xattn_body, bt=bt, tq=tq),
        grid=(B // bt, nq),
        in_specs=[qo, kv(kcol), kv(vcol)],
        out_specs=qo,
        out_shape=jax.ShapeDtypeStruct((B * L, D_MODEL), BF16),
        compiler_params=_cparams(2),
        name="xattn",
    )(q, mem_k, mem_v)


def _group_cfg(B, L):
    c = math.gcd(CHUNK, L)
    if L >= CHUNK:
        return dict(c=c, tl=min(512, L), m_bt=2, m_bu=2, g_bt=1, r_bt=2, r_bu=2, xa_bt=1,
                    xa_tq=min(512, L), idt=BF16)
    return dict(c=c, tl=L, m_bt=8, m_bu=4, g_bt=8, r_bt=8, r_bu=4, xa_bt=8, xa_tq=L, idt=F32)


def _layer(x, kv, state_in, l_in, prev, l, w, *, B, L, last_g):
    cfg = _group_cfg(B, L)
    c, tl, idt = cfg["c"], cfg["tl"], cfg["idt"]
    M = B * L
    conv0s, C0s, n0s, m0s, h0s, S0s = state_in
    C_prev, S_prev = prev if prev is not None else (None, None)

    x1, xn = ffn(x, w["f1_n"], w["f1_gu"], w["f1_d"], next_g=w["mix_n"], norm_dtype=BF16)

    conv2, conv_new = inproj_conv(xn, w["w_in_conv"], conv0s[l_in], w["conv_w"], w["conv_b"], B=B, L=L)
    mqkv = matmul(xn, w["w_in_mqkv"], idt, tm=1024, tn=1536)
    gates = matmul(xn, w["w_in_gates"], F32, tm=1024, tn=1536)
    small4 = matmul_t(w["w_in_small_t"], xn, tm=512).reshape(SMALL_PAD, B, L // c, c)
    conv3 = conv2.reshape(B, L, CONV_CH)
    mqkv3 = mqkv.reshape(B, L, 3 * D_MODEL)
    gates3 = gates.reshape(B, L, 3 * D_MODEL)

    hr, h1 = rglru(conv3, gates3, w["r_wa"], w["r_wx"], w["r_ba"], w["r_bx"], w["r_lam"],
                   h0s[l_in], tl=tl, out_dtype=idt)

    hm, C_out, n1, m1b = mlstm(mqkv3, gates3, small4, C0s, n0s, m0s, l_in, w["m_ib"], w["m_fb"],
                               w["m_norm"], C_prev, l, c=c, bt=cfg["m_bt"], bu=cfg["m_bu"], tl=tl,
                               out_dtype=idt)

    lm, qk, vb, kbg, qe, kd, eg = gdn_prep(conv3, small4, w["g_alog"], w["g_dtb"],
                                           c=c, bt=cfg["g_bt"], tl=tl, idt=idt)
    nchunks = M // c
    lt = lm.reshape(G_HEADS, nchunks, c, c).transpose(2, 3, 0, 1).reshape(c, c, G_HEADS * nchunks)
    tt = tri_inverse(lt)
    T = tt.reshape(c, c, G_HEADS, nchunks).transpose(2, 3, 0, 1).reshape(G_HEADS, B, L, c).astype(idt)
    hd, S_out = gdn_rec(T, qk, vb, kbg, qe, kd, eg, gates3, S0s, l_in, w["g_norm"], S_prev, l,
                        c=c, bt=cfg["r_bt"], bu=cfg["r_bu"], tl=tl, out_dtype=idt)

    x2, q = merge(x1, xn, hm.reshape(M, D_MODEL), hr.reshape(M, D_MODEL), hd.reshape(M, D_MODEL),
                  w["w_gate"], w["b_gate"], w["w_branch"], w["w_out"], w["xa_n"], w["xa_wq"])
    mem_k, mem_v, lk, kcol, vcol = kv
    o = xattn(q, mem_k, mem_v, lk, kcol, vcol, B=B, L=L, bt=cfg["xa_bt"], tq=cfg["xa_tq"])
    if last_g is None:
        x3, y = ffn(x2, w["f2_n"], w["f2_gu"], w["f2_d"], proj=(o, w["xa_wo"]))
    else:
        x3, y = ffn(x2, w["f2_n"], w["f2_gu"], w["f2_d"], proj=(o, w["xa_wo"]),
                    next_g=last_g, norm_dtype=F32)
    small_state = (conv_new, n1.reshape(B, M_HEADS, M_DH), m1b[:, :, 0, 0], h1)
    return x3, y, small_state, (C_out, S_out)


def _prep_weights(l, ffn1_norm, ffn1_w_gu, ffn1_w_down, mix_norm, w_in, conv_w, conv_b, m_igate_b,
                  m_fgate_b, m_norm, r_wa, r_ba, r_wx, r_bx, r_lambda, g_a_log, g_dt_bias, g_norm,
                  w_gate, b_gate, w_branch, w_out, xa_norm, xa_wq, xa_wo, ffn2_norm, ffn2_w_gu,
                  ffn2_w_down):
    row = lambda a: a[l].reshape(1, -1).astype(F32)
    b16 = lambda a: a[l].astype(BF16)
    small_t = jnp.zeros((SMALL_PAD, D_MODEL), BF16).at[:N_SMALL].set(
        w_in[l][:, OFF_SMALL:OFF_SMALL + N_SMALL].T.astype(BF16))
    return dict(
        f1_n=row(ffn1_norm), f1_gu=b16(ffn1_w_gu), f1_d=b16(ffn1_w_down), mix_n=row(mix_norm),
        w_in_conv=w_in[l][:, :CONV_CH].astype(BF16),
        w_in_mqkv=w_in[l][:, OFF_MQKV:OFF_GATES].astype(BF16),
        w_in_gates=w_in[l][:, OFF_GATES:OFF_SMALL].astype(BF16),
        w_in_small_t=small_t,
        conv_w=conv_w[l].astype(F32), conv_b=row(conv_b),
        m_ib=m_igate_b[l].astype(F32), m_fb=m_fgate_b[l].astype(F32), m_norm=row(m_norm),
        r_wa=b16(r_wa), r_wx=b16(r_wx), r_ba=row(r_ba), r_bx=row(r_bx), r_lam=row(r_lambda),
        g_alog=g_a_log[l].astype(F32), g_dtb=g_dt_bias[l].astype(F32), g_norm=row(g_norm),
        w_gate=b16(w_gate), b_gate=row(b_gate), w_branch=b16(w_branch), w_out=b16(w_out),
        xa_n=row(xa_norm), xa_wq=b16(xa_wq), xa_wo=b16(xa_wo),
        f2_n=row(ffn2_norm), f2_gu=b16(ffn2_w_gu), f2_d=b16(ffn2_w_down))


def _stacked_state(conv, C, n, m, h, S):
    Dp, B = C.shape[:2]
    return (conv, C, n.reshape(Dp, B, M_HEADS, 1, M_DH),
            jnp.broadcast_to(m[..., None, None], (Dp, B, M_HEADS, 1, LANE)), h, S)


def kernel(x_prompt, x_sample, cache_mem_k, cache_mem_v, state_conv, state_mlstm_C, state_mlstm_n, state_mlstm_m, state_rglru_h, state_delta_S, mem_prompt, ffn1_norm, ffn1_w_gu, ffn1_w_down, mix_norm, w_in, conv_w, conv_b, m_igate_b, m_fgate_b, m_norm, r_wa, r_ba, r_wx, r_bx, r_lambda, g_a_log, g_dt_bias, g_norm, w_gate, b_gate, w_branch, w_out, xa_norm, xa_wq, xa_wk, xa_wv, xa_wo, ffn2_norm, ffn2_w_gu, ffn2_w_down, final_norm):
    Bp, Lp, D = x_prompt.shape
    Bs, Ls, _ = x_sample.shape
    stacked = (ffn1_norm, ffn1_w_gu, ffn1_w_down, mix_norm, w_in, conv_w, conv_b, m_igate_b,
               m_fgate_b, m_norm, r_wa, r_ba, r_wx, r_bx, r_lambda, g_a_log, g_dt_bias, g_norm,
               w_gate, b_gate, w_branch, w_out, xa_norm, xa_wq, xa_wo, ffn2_norm, ffn2_w_gu,
               ffn2_w_down)
    fin = final_norm.reshape(1, D).astype(F32)
    memp = mem_prompt.reshape(Bp * N_MEM, D).astype(BF16)
    p_state = _stacked_state(jnp.zeros((1, Bp, CONV_W - 1, CONV_CH), F32),
                             jnp.zeros((1, Bp, M_HEADS, M_DH, M_DH), F32),
                             jnp.zeros((1, Bp, M_HEADS, M_DH), F32),
                             jnp.full((1, Bp, M_HEADS), M_INIT, F32),
                             jnp.zeros((1, Bp, R_WIDTH), F32),
                             jnp.zeros((1, Bp, G_HEADS, G_DH, G_DH), F32))
    s_state = _stacked_state(state_conv, state_mlstm_C, state_mlstm_n, state_mlstm_m,
                             state_rglru_h, state_delta_S)

    yp = x_prompt.reshape(Bp * Lp, D)
    ys = x_sample.reshape(Bs * Ls, D)
    p_mk, p_mv, p_small, s_small = [], [], [], []
    p_big = s_big = None
    outp = outs = None
    for l in range(DEPTH):
        w = _prep_weights(l, *stacked)
        last_g = fin if l == DEPTH - 1 else None
        wkv = jnp.concatenate([xa_wk[l], xa_wv[l]], axis=1).astype(BF16)
        mkv = matmul(memp, wkv, F32, tm=512, tn=1024)
        p_mk.append(mkv[:, :D].reshape(Bp, N_MEM, X_HEADS, X_DH))
        p_mv.append(mkv[:, D:].reshape(Bp, N_MEM, X_HEADS, X_DH))
        kv_p = (mkv.reshape(1, Bp, N_MEM, 2 * D), mkv.reshape(1, Bp, N_MEM, 2 * D), 0, 0, 1)
        yp, outp, sm, p_big = _layer(yp, kv_p, p_state, 0, p_big, l, w, B=Bp, L=Lp, last_g=last_g)
        p_small.append(sm)
        ys, outs, sm, s_big = _layer(ys, (cache_mem_k, cache_mem_v, l, 0, 0), s_state, l, s_big, l, w,
                                     B=Bs, L=Ls, last_g=last_g)
        s_small.append(sm)

    def assemble(small, big):
        conv, n, m, h = [jnp.stack([sm[i] for sm in small]) for i in range(4)]
        C, S = big
        return conv, C, n, m, h, S

    y_prompt = outp.reshape(Bp, Lp, D)
    y_sample = outs.reshape(Bs, Ls, D)
    return (y_prompt, y_sample, jnp.stack(p_mk), jnp.stack(p_mv),
            *assemble(p_small, p_big), *assemble(s_small, s_big))
```

```python
import functools
import math

import jax
import jax.numpy as jnp
from jax import lax
from jax.experimental import pallas as pl
from jax.experimental.pallas import tpu as pltpu

F32 = jnp.float32
BF16 = jnp.bfloat16

D_MODEL = 1024
DEPTH = 4
N_MEM = 256
X_HEADS = 4
X_DH = D_MODEL // X_HEADS
M_HEADS = 4
M_DH = D_MODEL // M_HEADS
R_WIDTH = D_MODEL
R_BLOCKS = 8
R_BDIM = R_WIDTH // R_BLOCKS
R_C = 8.0
G_HEADS = 8
G_DH = D_MODEL // G_HEADS
N_BRANCH = 3
CONV_W = 4
CHUNK = 64
D_FF = 2816
EPS = 1e-6
M_INIT = -1e30

CONV_CH = R_WIDTH + 3 * D_MODEL
OFF_MQKV = CONV_CH
OFF_GATES = OFF_MQKV + 3 * D_MODEL
OFF_SMALL = OFF_GATES + 3 * D_MODEL
N_SMALL = 2 * M_HEADS + 2 * G_HEADS
SMALL_PAD = 32
SM_MI, SM_MF = 0, M_HEADS
SM_GB_BLK, SM_GA_BLK = 1, 2

CONV_SUB = 512
R_GROUP = 1
LANE = 128
SUBLANE = 8
VMEM_LIMIT = 52 * 1024 * 1024


def _cparams(n_axes, vmem=VMEM_LIMIT):
    return pltpu.CompilerParams(dimension_semantics=("arbitrary",) * n_axes,
                                vmem_limit_bytes=vmem)


def _resident(shape, index_map):
    return pl.BlockSpec(shape, index_map, pipeline_mode=pl.Buffered(1))


_SMEM = pl.BlockSpec(memory_space=pltpu.SMEM)
_ANY = pl.BlockSpec(memory_space=pl.ANY)


def _rms(x, g):
    return x * lax.rsqrt(jnp.mean(x * x, axis=-1, keepdims=True) + EPS) * g


def _softplus(x):
    return jnp.maximum(x, 0.0) + jnp.log1p(jnp.exp(-jnp.abs(x)))


def _sigmoid_t(x):
    return 0.5 * jnp.tanh(0.5 * x) + 0.5


def _l2norm(x):
    return x * lax.rsqrt(jnp.sum(x * x, axis=-1, keepdims=True) + EPS)


def _silu(x):
    return x * jax.nn.sigmoid(x)


def _dot(a, b):
    return jnp.dot(a, b, preferred_element_type=F32)


def _dot_nt(a, b):
    return lax.dot_general(a, b, (((1,), (1,)), ((), ())), preferred_element_type=F32)


def _dot_tn(a, b):
    return lax.dot_general(a, b, (((0,), (0,)), ((), ())), preferred_element_type=F32)


def _mm_body(x_ref, w_ref, o_ref):
    o_ref[...] = _dot(x_ref[...], w_ref[...]).astype(o_ref.dtype)


def matmul(x, w, out_dtype, tm, tn):
    M, K = x.shape
    N = w.shape[1]
    tm = min(tm, M)
    return pl.pallas_call(
        _mm_body,
        grid=(N // tn, M // tm),
        in_specs=[pl.BlockSpec((tm, K), lambda j, i: (i, 0)),
                  pl.BlockSpec((K, tn), lambda j, i: (0, j))],
        out_specs=pl.BlockSpec((tm, tn), lambda j, i: (i, j)),
        out_shape=jax.ShapeDtypeStruct((M, N), out_dtype),
        compiler_params=_cparams(2),
        name="matmul",
    )(x, w)


def _mm_t_body(w_ref, x_ref, o_ref):
    o_ref[...] = _dot_nt(w_ref[...], x_ref[...])


def matmul_t(wt, x, tm):
    P, K = wt.shape
    M = x.shape[0]
    tm = min(tm, M)
    return pl.pallas_call(
        _mm_t_body,
        grid=(M // tm,),
        in_specs=[pl.BlockSpec((P, K), lambda i: (0, 0)),
                  pl.BlockSpec((tm, K), lambda i: (i, 0))],
        out_specs=pl.BlockSpec((P, tm), lambda i: (0, i)),
        out_shape=jax.ShapeDtypeStruct((P, M), F32),
        compiler_params=_cparams(1),
        name="matmul_t",
    )(wt, x)


def _inproj_conv_body(x_ref, w_ref, c0_ref, cw_ref, cb_ref, o_ref, cn_ref, s_ref, *,
                      bt, Lt, seq_tiles):
    tn = w_ref.shape[1]
    pad = SUBLANE
    lo = pad - (CONV_W - 1)
    if seq_tiles == 1:
        s_ref[:, lo:pad, :] = c0_ref[...]
    else:
        first = (pl.program_id(1) % seq_tiles) == 0

        @pl.when(first)
        def _():
            s_ref[:, lo:pad, :] = c0_ref[...]

        @pl.when(jnp.logical_not(first))
        def _():
            s_ref[:, lo:pad, :] = s_ref[:, Lt + lo:Lt + pad, :]

    x = x_ref[...]
    for c0 in range(0, tn, CONV_SUB):
        cs = slice(c0, c0 + CONV_SUB)
        pc = _dot(x, w_ref[:, cs]).reshape(bt, Lt, CONV_SUB)
        s_ref[:, pad:pad + Lt, cs] = pc
        acc = cb_ref[:, cs][None] + pc * cw_ref[CONV_W - 1:CONV_W, cs][None]
        for j in range(CONV_W - 1):
            acc = acc + s_ref[:, lo + j:lo + j + Lt, cs] * cw_ref[j:j + 1, cs][None]
        o_ref[:, cs] = acc.reshape(bt * Lt, CONV_SUB)
    cn_ref[...] = s_ref[:, Lt + lo:Lt + pad, :]


def inproj_conv(xn, w, conv0, cw, cb, *, B, L, tm=512, tn=2048):
    M, K = xn.shape
    tm = min(tm, M)
    if L >= tm:
        bt, Lt, seq_tiles = 1, tm, L // tm
        seq = lambda j, i: (i // seq_tiles, 0, j)
    else:
        bt, Lt, seq_tiles = tm // L, L, 1
        seq = lambda j, i: (i, 0, j)
    return pl.pallas_call(
        functools.partial(_inproj_conv_body, bt=bt, Lt=Lt, seq_tiles=seq_tiles),
        grid=(CONV_CH // tn, M // tm),
        in_specs=[pl.BlockSpec((tm, K), lambda j, i: (i, 0)),
                  pl.BlockSpec((K, tn), lambda j, i: (0, j)),
                  pl.BlockSpec((bt, CONV_W - 1, tn), seq),
                  pl.BlockSpec((CONV_W, tn), lambda j, i: (0, j)),
                  pl.BlockSpec((1, tn), lambda j, i: (0, j))],
        out_specs=[pl.BlockSpec((tm, tn), lambda j, i: (i, j)),
                   pl.BlockSpec((bt, CONV_W - 1, tn), seq)],
        out_shape=[jax.ShapeDtypeStruct((M, CONV_CH), F32),
                   jax.ShapeDtypeStruct((B, CONV_W - 1, CONV_CH), F32)],
        scratch_shapes=[pltpu.VMEM((bt, Lt + SUBLANE, tn), F32)],
        compiler_params=_cparams(2),
        name="inproj_conv",
    )(xn, w, conv0, cw, cb)


def _ffn_body(*refs, has_proj, norm_dtype):
    it = iter(refs)
    x_ref = next(it)
    if has_proj:
        o_ref, wo_ref = next(it), next(it)
    g_ref, wgu_ref, wd_ref = next(it), next(it), next(it)
    if norm_dtype is not None:
        g2_ref = next(it)
    xo_ref = next(it)
    if norm_dtype is not None:
        no_ref = next(it)
    x = x_ref[...]
    if has_proj:
        x = x + _dot(o_ref[...], wo_ref[...])
    xn = _rms(x, g_ref[...]).astype(BF16)
    gu = _dot(xn, wgu_ref[...])
    g = gu[:, :D_FF]
    u = gu[:, D_FF:]
    a = (g * jax.nn.sigmoid(g) * u).astype(BF16)
    y = x + 0.5 * _dot(a, wd_ref[...])
    xo_ref[...] = y
    if norm_dtype is not None:
        no_ref[...] = _rms(y, g2_ref[...]).astype(norm_dtype)


def ffn(x, norm_g, w_gu, w_d, *, proj=None, next_g=None, norm_dtype=None, tm=512):
    M, D = x.shape
    tm = min(tm, M)
    row = lambda i: (i, 0)
    fixed = lambda i: (0, 0)
    args = [x]
    in_specs = [pl.BlockSpec((tm, D), row)]
    if proj is not None:
        o, wo = proj
        args += [o, wo]
        in_specs += [pl.BlockSpec((tm, D), row), _resident((D, D), fixed)]
    args += [norm_g, w_gu, w_d]
    in_specs += [_resident((1, D), fixed), _resident(w_gu.shape, fixed), _resident(w_d.shape, fixed)]
    out_shape = [jax.ShapeDtypeStruct((M, D), F32)]
    out_specs = [pl.BlockSpec((tm, D), row)]
    if norm_dtype is not None:
        args.append(next_g)
        in_specs.append(_resident((1, D), fixed))
        out_shape.append(jax.ShapeDtypeStruct((M, D), norm_dtype))
        out_specs.append(pl.BlockSpec((tm, D), row))
    res = pl.pallas_call(
        functools.partial(_ffn_body, has_proj=proj is not None, norm_dtype=norm_dtype),
        grid=(M // tm,),
        in_specs=in_specs, out_specs=out_specs, out_shape=out_shape,
        compiler_params=_cparams(1),
        name="ffn",
    )(*args)
    return res if norm_dtype is not None else (res[0], None)


def _rglru_body(rx_ref, ry_ref, wa_ref, wx_ref, ba_ref, bx_ref, lam_ref, h0_ref,
                hr_ref, h1_ref, hc_s, *ab_s, B, tl):
    R = B * tl
    W = R_GROUP * R_BDIM

    @pl.when(pl.program_id(1) == 0)
    def _():
        hc_s[...] = h0_ref[...]

    rx = rx_ref[...].reshape(R, W)
    rxb = rx.astype(BF16)
    blocks = [slice(g * R_BDIM, (g + 1) * R_BDIM) for g in range(R_GROUP)]
    pre_a = jnp.concatenate([_dot(rxb[:, cs], wa_ref[g]) for g, cs in enumerate(blocks)], axis=1)
    pre_x = jnp.concatenate([_dot(rxb[:, cs], wx_ref[g]) for g, cs in enumerate(blocks)], axis=1)
    r = _sigmoid_t(pre_a + ba_ref[...])
    ig = _sigmoid_t(pre_x + bx_ref[...])
    log_a = -R_C * r * _softplus(-lam_ref[...])
    a = jnp.exp(log_a)
    bx = jnp.sqrt(jnp.tanh(-log_a) * (a * a + 1.0)) * (ig * rx)
    a_s, b_s = ab_s[:R_GROUP], ab_s[R_GROUP:]
    if tl > SUBLANE and R_GROUP == 1:
        G = R // SUBLANE
        row = lax.broadcasted_iota(jnp.int32, (G, SUBLANE, W), 1)
        A, Bv = a.reshape(G, SUBLANE, W), bx.reshape(G, SUBLANE, W)
        for s in (1, 2, 4):
            keep = row >= s
            Bv = jnp.where(keep, A * pltpu.roll(Bv, s, 1) + Bv, Bv)
            A = jnp.where(keep, A * pltpu.roll(A, s, 1), A)
        a_s[0][...] = A.reshape(R, W)
        b_s[0][...] = Bv.reshape(R, W)

        def group(g, hs):
            new = []
            for b in range(B):
                rows = pl.ds(pl.multiple_of(b * tl + g * SUBLANE, SUBLANE), SUBLANE)
                hh = a_s[0][rows, :] * hs[b] + b_s[0][rows, :]
                a_s[0][rows, :] = hh
                new.append(hh[SUBLANE - 1:SUBLANE, :])
            return tuple(new)

        hs = lax.fori_loop(0, tl // SUBLANE, group, tuple(hc_s[b:b + 1, :] for b in range(B)))
        h = jnp.concatenate(hs, axis=0)
    else:
        for g, cs in enumerate(blocks):
            a_s[g][...] = a[:, cs]
            b_s[g][...] = bx[:, cs]

        def step(t, hs):
            rows = pl.ds(t, B, stride=tl)
            new = []
            for g in range(R_GROUP):
                h = a_s[g][rows, :] * hs[g] + b_s[g][rows, :]
                a_s[g][rows, :] = h
                new.append(h)
            return tuple(new)

        hs = lax.fori_loop(0, tl, step, tuple(hc_s[:, cs] for cs in blocks), unroll=SUBLANE)
        h = jnp.concatenate(hs, axis=1)
    hc_s[...] = h
    h1_ref[...] = h
    hall = jnp.concatenate([a_s[g][...] for g in range(R_GROUP)], axis=1)
    hr = hall * jax.nn.gelu(ry_ref[...].reshape(R, W))
    hr_ref[...] = hr.reshape(B, tl, W).astype(hr_ref.dtype)


def rglru(conv3, gates3, wa, wx, ba, bx, lam, h0, *, tl, out_dtype):
    B, L, _ = conv3.shape
    W = R_GROUP * R_BDIM
    ry_blk = D_MODEL // W
    vec = lambda n, j: (0, n)
    return pl.pallas_call(
        functools.partial(_rglru_body, B=B, tl=tl),
        grid=(R_BLOCKS // R_GROUP, L // tl),
        in_specs=[pl.BlockSpec((B, tl, W), lambda n, j: (0, j, n)),
                  pl.BlockSpec((B, tl, W), lambda n, j: (0, j, ry_blk + n)),
                  pl.BlockSpec((R_GROUP, R_BDIM, R_BDIM), lambda n, j: (n, 0, 0)),
                  pl.BlockSpec((R_GROUP, R_BDIM, R_BDIM), lambda n, j: (n, 0, 0)),
                  pl.BlockSpec((1, W), vec),
                  pl.BlockSpec((1, W), vec),
                  pl.BlockSpec((1, W), vec),
                  pl.BlockSpec((B, W), vec)],
        out_specs=[pl.BlockSpec((B, tl, W), lambda n, j: (0, j, n)),
                   pl.BlockSpec((B, W), vec)],
        out_shape=[jax.ShapeDtypeStruct((B, L, R_WIDTH), out_dtype),
                   jax.ShapeDtypeStruct((B, R_WIDTH), F32)],
        scratch_shapes=[pltpu.VMEM((B, W), F32)]
        + [pltpu.VMEM((B * tl, R_BDIM), F32)] * (2 * R_GROUP),
        compiler_params=_cparams(2),
        name="rglru",
    )(conv3, gates3, wa, wx, ba, bx, lam, h0)


def _chunk_masks(c):
    t = lax.broadcasted_iota(jnp.int32, (c, c), 0)
    s = lax.broadcasted_iota(jnp.int32, (c, c), 1)
    return s <= t, s == t, s < t


def _col_from_row(row, eye):
    return jnp.sum(jnp.where(eye, row, 0.0), axis=1, keepdims=True)


def _row_from_col(col, eye):
    return jnp.sum(jnp.where(eye, col, 0.0), axis=0, keepdims=True)


def _split_iter(it, ncl):
    if ncl == 1:
        return it, 0
    return it // ncl, it % ncl


def _chunk_rows(ci, c):
    if isinstance(ci, int):
        return pl.ds(ci * c, c)
    return pl.ds(pl.multiple_of(ci * c, c), c)


def _mlstm_body(*refs, bt, bu, tl, c, aliased, carried):
    (q_ref, k_ref, v_ref, mo_ref, sm_ref, c0_ref, n0_ref, m0_ref, ib_ref, fb_ref, g_ref) = refs[:11]
    hm_ref, c1_ref, n1_ref, m1_ref = refs[11 + int(aliased):15 + int(aliased)]
    ncl = tl // c
    tri, eye, _ = _chunk_masks(c)
    scale = M_DH ** -0.5
    nch = bu * M_HEADS
    if carried:
        scr = refs[15 + int(aliased):]
        c_s, n_s, m_s = scr[:nch], scr[nch:2 * nch], scr[2 * nch:]
        j = pl.program_id(1)

        @pl.when(j == 0)
        def _():
            for bb in range(bu):
                for h in range(M_HEADS):
                    c_s[bb * M_HEADS + h][...] = c0_ref[bb, h]
                    n_s[bb * M_HEADS + h][...] = n0_ref[bb, h]
                    m_s[bb * M_HEADS + h][...] = m0_ref[bb, h]

    def chunk(it, carry):
        bg, ci = _split_iter(it, ncl)
        rows = _chunk_rows(ci, c)
        st = []
        for bb in range(bu):
            for h in range(M_HEADS):
                bi = bg * bu + bb
                d = dict(bi=bi, h=h, ch=bb * M_HEADS + h, cols=slice(h * M_DH, (h + 1) * M_DH))
                d["li_row"] = sm_ref[SM_MI + h, bi, pl.ds(ci, 1), :] + ib_ref[h]
                lf_row = -_softplus(-(sm_ref[SM_MF + h, bi, pl.ds(ci, 1), :] + fb_ref[h]))
                d["b_col"] = jnp.sum(jnp.where(tri, lf_row, 0.0), axis=1, keepdims=True)
                d["li_col"] = _col_from_row(d["li_row"], eye)
                st.append(d)
        for d in st:
            b_row = _row_from_col(d["b_col"], eye)
            d["logD"] = jnp.where(tri, d["b_col"] - b_row + d["li_row"], -jnp.inf)
            d["mx"] = jnp.max(d["logD"], axis=1, keepdims=True)
        for d in st:
            bi, h, ch, cols = d["bi"], d["h"], d["ch"], d["cols"]
            m = m_s[ch][:, 0:1] if carried else m0_ref[bi, h][:, 0:1]
            b_col = d["b_col"]
            inter = b_col + m
            mt = jnp.maximum(inter, d["mx"])
            m_new = mt[c - 1:c, :]
            b_last = b_col[c - 1:c, :]
            kf = k_ref[bi, rows, cols].astype(F32) * scale
            d.update(kf=kf, mt=mt, m_new=m_new, dec=jnp.exp(b_last + m - m_new),
                     kw=kf * jnp.exp(b_last - b_col + d["li_col"] - m_new),
                     wi=jnp.exp(inter - mt), dmat=jnp.exp(d["logD"] - mt))
        for d in st:
            bi, h, ch, cols = d["bi"], d["h"], d["ch"], d["cols"]
            qc = q_ref[bi, rows, cols].astype(BF16)
            vc = v_ref[bi, rows, cols].astype(BF16)
            C, n = (c_s[ch][...], n_s[ch][...]) if carried else (c0_ref[bi, h], n0_ref[bi, h])
            d["qk"] = _dot_nt(qc, d["kf"].astype(BF16))
            d["qC"] = _dot(qc, C.astype(BF16))
            d["qn"] = _dot_nt(qc, jnp.broadcast_to(n, (SUBLANE, M_DH)).astype(BF16))[:, 0:1]
            C_new = d["dec"] * C + _dot_tn(d["kw"].astype(BF16), vc)
            n_new = d["dec"] * n + jnp.sum(d["kw"], axis=0, keepdims=True)
            m_row = jnp.broadcast_to(d["m_new"], (1, LANE))
            if carried:
                c_s[ch][...], n_s[ch][...], m_s[ch][...] = C_new, n_new, m_row
            else:
                c1_ref[bi, h], n1_ref[bi, h], m1_ref[bi, h] = C_new, n_new, m_row
            d["vc"] = vc
        for d in st:
            s = d["qk"] * d["dmat"]
            d["sv"] = _dot(s.astype(BF16), d["vc"])
            d["rowsum"] = jnp.sum(s, axis=1, keepdims=True)
        for d in st:
            num = d["sv"] + d["wi"] * d["qC"]
            den = d["rowsum"] + d["wi"] * d["qn"]
            d["hh"] = num / jnp.maximum(jnp.abs(den), jnp.exp(-d["mt"]))
        for d in st:
            d["ms"] = jnp.mean(d["hh"] * d["hh"], axis=1, keepdims=True)
        for d in st:
            bi, cols = d["bi"], d["cols"]
            y = (d["hh"] * lax.rsqrt(d["ms"] + EPS) * g_ref[:, cols]
                 * jax.nn.sigmoid(mo_ref[bi, rows, cols]))
            hm_ref[bi, rows, cols] = y.astype(hm_ref.dtype)
        return carry

    lax.fori_loop(0, (bt // bu) * ncl, chunk, 0)

    if carried:
        @pl.when(j == pl.num_programs(1) - 1)
        def _():
            for bb in range(bu):
                for h in range(M_HEADS):
                    c1_ref[bb, h] = c_s[bb * M_HEADS + h][...]
                    n1_ref[bb, h] = n_s[bb * M_HEADS + h][...]
                    m1_ref[bb, h] = m_s[bb * M_HEADS + h][...]


def mlstm(mqkv3, gates3, small4, C0s, n0s, m0s, l_in, ib, fb, gain, C_prev, l, *, c, bt, bu, tl,
          out_dtype):
    B, L, _ = mqkv3.shape
    ncl = tl // c
    tok = lambda cb: pl.BlockSpec((bt, tl, D_MODEL), lambda i, j: (i, j, cb))
    st_in = lambda i, j: (l_in, i, 0, 0, 0)
    st_out = lambda i, j: (l, i, 0, 0, 0)
    st = lambda i, j: (i, 0, 0, 0)
    in_specs = [tok(0), tok(1), tok(2), tok(0),
                pl.BlockSpec((SUBLANE, bt, ncl, c), lambda i, j: (0, i, j, 0)),
                pl.BlockSpec((None, bt, M_HEADS, M_DH, M_DH), st_in),
                pl.BlockSpec((None, bt, M_HEADS, 1, M_DH), st_in),
                pl.BlockSpec((None, bt, M_HEADS, 1, LANE), st_in),
                _SMEM, _SMEM,
                pl.BlockSpec((1, D_MODEL), lambda i, j: (0, 0))]
    args = [mqkv3, mqkv3, mqkv3, gates3, small4, C0s, n0s, m0s, ib, fb, gain]
    aliases = {}
    if C_prev is not None:
        in_specs.append(_ANY)
        args.append(C_prev)
        aliases = {len(args) - 1: 1}
    carried = L > c
    scratch = []
    if carried:
        assert bt == bu
        nch = bu * M_HEADS
        scratch = ([pltpu.VMEM((M_DH, M_DH), F32)] * nch + [pltpu.VMEM((1, M_DH), F32)] * nch
                   + [pltpu.VMEM((1, LANE), F32)] * nch)
    return pl.pallas_call(
        functools.partial(_mlstm_body, bt=bt, bu=bu, tl=tl, c=c, aliased=C_prev is not None,
                          carried=carried),
        grid=(B // bt, L // tl),
        scratch_shapes=scratch,
        in_specs=in_specs,
        out_specs=[pl.BlockSpec((bt, tl, D_MODEL), lambda i, j: (i, j, 0)),
                   pl.BlockSpec((None, bt, M_HEADS, M_DH, M_DH), st_out),
                   pl.BlockSpec((bt, M_HEADS, 1, M_DH), st),
                   pl.BlockSpec((bt, M_HEADS, 1, LANE), st)],
        out_shape=[jax.ShapeDtypeStruct((B, L, D_MODEL), out_dtype),
                   jax.ShapeDtypeStruct((DEPTH, B, M_HEADS, M_DH, M_DH), F32),
                   jax.ShapeDtypeStruct((B, M_HEADS, 1, M_DH), F32),
                   jax.ShapeDtypeStruct((B, M_HEADS, 1, LANE), F32)],
        input_output_aliases=aliases,
        compiler_params=_cparams(2),
        name="mlstm",
    )(*args)


def _gdn_prep_body(q_ref, k_ref, v_ref, sb_ref, sa_ref, alog_ref, dtb_ref,
                   lm_ref, qk_ref, vb_ref, kbg_ref, qe_ref, kd_ref, eg_ref, *, bt, tl, c):
    ncl = tl // c
    tri, eye, strict = _chunk_masks(c)

    def chunk(it, carry):
        bi, ci = _split_iter(it, ncl)
        rows = _chunk_rows(ci, c)
        crow = pl.ds(ci, 1)
        for h in range(G_HEADS):
            cols = slice(h * G_DH, (h + 1) * G_DH)
            q = _l2norm(_silu(q_ref[bi, rows, cols])) * (G_DH ** -0.5)
            k = _l2norm(_silu(k_ref[bi, rows, cols]))
            v = _silu(v_ref[bi, rows, cols])
            beta_row = jax.nn.sigmoid(sb_ref[h, bi, crow, :])
            g_row = -jnp.exp(alog_ref[h]) * _softplus(sa_ref[h, bi, crow, :] + dtb_ref[h])
            G_col = jnp.sum(jnp.where(tri, g_row, 0.0), axis=1, keepdims=True)
            G_row = _row_from_col(G_col, eye)
            beta_col = _col_from_row(beta_row, eye)
            dec = jnp.exp(jnp.where(tri, G_col - G_row, -jnp.inf))
            kb16 = k.astype(BF16)
            kk = _dot_nt(kb16, kb16)
            lm_ref[h, bi, rows, :] = jnp.where(strict, beta_col * kk * dec, 0.0)
            qk_ref[h, bi, rows, :] = (_dot_nt(q.astype(BF16), kb16) * dec).astype(qk_ref.dtype)
            eG = jnp.exp(G_col)
            vb_ref[bi, rows, cols] = (v * beta_col).astype(vb_ref.dtype)
            kbg_ref[bi, rows, cols] = (k * beta_col * eG).astype(kbg_ref.dtype)
            qe_ref[bi, rows, cols] = (q * eG).astype(qe_ref.dtype)
            gl = G_col[c - 1:c, :]
            kd_ref[bi, rows, cols] = (k * jnp.exp(gl - G_col)).astype(kd_ref.dtype)
            eg_ref[h, bi, crow, :] = jnp.exp(G_row)
        return carry

    lax.fori_loop(0, bt * ncl, chunk, 0)


def gdn_prep(conv3, small4, alog, dtb, *, c, bt, tl, idt):
    B, L, _ = conv3.shape
    ncl = tl // c
    tok_in = lambda cb: pl.BlockSpec((bt, tl, D_MODEL), lambda i, j: (i, j, cb))
    tok = pl.BlockSpec((bt, tl, D_MODEL), lambda i, j: (i, j, 0))
    small = lambda blk: pl.BlockSpec((SUBLANE, bt, ncl, c), lambda i, j: (blk, i, j, 0))
    per_head = pl.BlockSpec((G_HEADS, bt, tl, c), lambda i, j: (0, i, j, 0))
    return pl.pallas_call(
        functools.partial(_gdn_prep_body, bt=bt, tl=tl, c=c),
        grid=(B // bt, L // tl),
        in_specs=[tok_in(1), tok_in(2), tok_in(3), small(SM_GB_BLK), small(SM_GA_BLK), _SMEM, _SMEM],
        out_specs=[per_head, per_head, tok, tok, tok, tok, small(0)],
        out_shape=[jax.ShapeDtypeStruct((G_HEADS, B, L, c), F32),
                   jax.ShapeDtypeStruct((G_HEADS, B, L, c), idt),
                   jax.ShapeDtypeStruct((B, L, D_MODEL), idt),
                   jax.ShapeDtypeStruct((B, L, D_MODEL), idt),
                   jax.ShapeDtypeStruct((B, L, D_MODEL), idt),
                   jax.ShapeDtypeStruct((B, L, D_MODEL), idt),
                   jax.ShapeDtypeStruct((G_HEADS, B, L // c, c), F32)],
        compiler_params=_cparams(2),
        name="gdn_prep",
    )(conv3, conv3, conv3, small4, small4, alog, dtb)


def _fsub_body(l_ref, x_ref, *, c):
    jrow = lax.broadcasted_iota(jnp.int32, (SUBLANE, LANE), 0)
    x_ref[...] = jnp.zeros(x_ref.shape, F32)
    grp = lambda g: slice(g * SUBLANE, (g + 1) * SUBLANE)

    for tb in range(c // SUBLANE):
        def r_body(r, carry, tb=tb):
            t = tb * SUBLANE + r
            acc = [jnp.zeros((SUBLANE, LANE), F32) for _ in range(tb)]
            acc.append(jnp.where(jrow == r, 1.0, 0.0))
            for sb in range(tb + 1):
                lblk = l_ref[t, grp(sb), :]
                for kk in range(SUBLANE):
                    lrow = lblk[kk:kk + 1, :]
                    for jg in range(sb + 1):
                        acc[jg] = acc[jg] - lrow * x_ref[sb * SUBLANE + kk, grp(jg), :]
            for jg in range(tb + 1):
                x_ref[t, grp(jg), :] = acc[jg]
            return carry

        lax.fori_loop(0, SUBLANE, r_body, 0)


def tri_inverse(lt):
    c, _, n = lt.shape
    blk = pl.BlockSpec((c, c, LANE), lambda i: (0, 0, i))
    return pl.pallas_call(
        functools.partial(_fsub_body, c=c),
        grid=(n // LANE,),
        in_specs=[blk], out_specs=blk,
        out_shape=jax.ShapeDtypeStruct((c, c, n), F32),
        compiler_params=_cparams(1),
        name="tri_inverse",
    )(lt)


def _gdn_rec_body(*refs, bt, bu, tl, c, aliased, carried):
    (t_ref, qk_ref, vb_ref, kbg_ref, qe_ref, kd_ref, eg_ref, gz_ref, s0_ref, g_ref) = refs[:10]
    hd_ref, s1_ref = refs[10 + int(aliased):12 + int(aliased)]
    ncl = tl // c
    gain = g_ref[...]
    chains = [(bb, h) for bb in range(bu) for h in range(G_HEADS)]
    if carried:
        s_s = refs[12 + int(aliased):]
        j = pl.program_id(1)

        @pl.when(j == 0)
        def _():
            for k, (bb, h) in enumerate(chains):
                s_s[k][...] = s0_ref[bb, h]

    def chunk(it, carry):
        bg, ci = _split_iter(it, ncl)
        rows = _chunk_rows(ci, c)
        ch = [(bg * bu + bb, h, slice(h * G_DH, (h + 1) * G_DH)) for bb, h in chains]
        uw = []
        for bi, h, cols in ch:
            T = t_ref[h, bi, rows, :].astype(BF16)
            rhs = jnp.concatenate([vb_ref[bi, rows, cols], kbg_ref[bi, rows, cols]], axis=1)
            uw.append(_dot(T, rhs.astype(BF16)))
        ws = []
        for k, (bi, h, cols) in enumerate(ch):
            S = s_s[k][...] if carried else s0_ref[bi, h]
            lhs = jnp.concatenate([uw[k][:, G_DH:].astype(qe_ref.dtype), qe_ref[bi, rows, cols]],
                                  axis=0)
            ws.append(_dot(lhs.astype(BF16), S.astype(BF16)))
        os_ = []
        for k, (bi, h, cols) in enumerate(ch):
            u = (uw[k][:, :G_DH] - ws[k][:c]).astype(BF16)
            os_.append(ws[k][c:] + _dot(qk_ref[h, bi, rows, :].astype(BF16), u))
            dS = _dot_tn(kd_ref[bi, rows, cols].astype(BF16), u)
            egl = eg_ref[h, bi, pl.ds(ci, 1), :][:, c - 1:c]
            if carried:
                s_s[k][...] = egl * s_s[k][...] + dS
            else:
                s1_ref[bi, h] = egl * s0_ref[bi, h] + dS
        ms = [jnp.mean(o * o, axis=-1, keepdims=True) for o in os_]
        for k, (bi, h, cols) in enumerate(ch):
            y = os_[k] * lax.rsqrt(ms[k] + EPS) * gain * _silu(gz_ref[bi, rows, cols])
            hd_ref[bi, rows, cols] = y.astype(hd_ref.dtype)
        return carry

    lax.fori_loop(0, (bt // bu) * ncl, chunk, 0)

    if carried:
        @pl.when(j == pl.num_programs(1) - 1)
        def _():
            for k, (bb, h) in enumerate(chains):
                s1_ref[bb, h] = s_s[k][...]


def gdn_rec(T, qk, vb, kbg, qe, kd, eg, gates3, S0s, l_in, gain, S_prev, l, *, c, bt, bu, tl,
            out_dtype):
    B, L, _ = vb.shape
    ncl = tl // c
    tok = pl.BlockSpec((bt, tl, D_MODEL), lambda i, j: (i, j, 0))
    per_head = pl.BlockSpec((G_HEADS, bt, tl, c), lambda i, j: (0, i, j, 0))
    in_specs = [per_head, per_head, tok, tok, tok, tok,
                pl.BlockSpec((G_HEADS, bt, ncl, c), lambda i, j: (0, i, j, 0)),
                pl.BlockSpec((bt, tl, D_MODEL), lambda i, j: (i, j, 2)),
                pl.BlockSpec((None, bt, G_HEADS, G_DH, G_DH), lambda i, j: (l_in, i, 0, 0, 0)),
                pl.BlockSpec((1, G_DH), lambda i, j: (0, 0))]
    args = [T, qk, vb, kbg, qe, kd, eg, gates3, S0s, gain]
    aliases = {}
    if S_prev is not None:
        in_specs.append(_ANY)
        args.append(S_prev)
        aliases = {len(args) - 1: 1}
    carried = L > c
    if carried:
        assert bt == bu
    scratch = [pltpu.VMEM((G_DH, G_DH), F32)] * (bu * G_HEADS) if carried else []
    return pl.pallas_call(
        functools.partial(_gdn_rec_body, bt=bt, bu=bu, tl=tl, c=c, aliased=S_prev is not None,
                          carried=carried),
        grid=(B // bt, L // tl),
        scratch_shapes=scratch,
        in_specs=in_specs,
        out_specs=[tok, pl.BlockSpec((None, bt, G_HEADS, G_DH, G_DH), lambda i, j: (l, i, 0, 0, 0))],
        out_shape=[jax.ShapeDtypeStruct((B, L, D_MODEL), out_dtype),
                   jax.ShapeDtypeStruct((DEPTH, B, G_HEADS, G_DH, G_DH), F32)],
        input_output_aliases=aliases,
        compiler_params=_cparams(2),
        name="gdn_rec",
    )(*args)


def _merge_body(x_ref, xn_ref, hm_ref, hr_ref, hd_ref, wg_ref, bg_ref, wb_ref, wo_ref,
                gx_ref, wq_ref, xo_ref, q_ref):
    xn = xn_ref[...]
    merged = None
    for n, br in enumerate((hm_ref, hr_ref, hd_ref)):
        lo = n * D_MODEL
        gate = jax.nn.sigmoid(_dot(xn, wg_ref[:, lo:lo + D_MODEL]) + bg_ref[:, lo:lo + D_MODEL])
        term = gate * _dot(br[...].astype(BF16), wb_ref[n])
        merged = term if merged is None else merged + term
    x = x_ref[...] + _dot(merged.astype(BF16), wo_ref[...])
    xo_ref[...] = x
    q_ref[...] = _dot(_rms(x, gx_ref[...]).astype(BF16), wq_ref[...]).astype(q_ref.dtype)


def merge(x, xn, hm, hr, hd, w_gate, b_gate, w_branch, w_out, xa_g, xa_wq, *, tm=512):
    M, D = x.shape
    tm = min(tm, M)
    row = pl.BlockSpec((tm, D), lambda i: (i, 0))
    f2 = lambda i: (0, 0)
    return pl.pallas_call(
        _merge_body,
        grid=(M // tm,),
        in_specs=[row, row, row, row, row,
                  _resident(w_gate.shape, f2), _resident(b_gate.shape, f2),
                  _resident(w_branch.shape, lambda i: (0, 0, 0)), _resident(w_out.shape, f2),
                  _resident(xa_g.shape, f2), _resident(xa_wq.shape, f2)],
        out_specs=[row, row],
        out_shape=[jax.ShapeDtypeStruct((M, D), F32), jax.ShapeDtypeStruct((M, D), F32)],
        compiler_params=_cparams(1),
        name="merge",
    )(x, xn, hm, hr, hd, w_gate, b_gate, w_branch, w_out, xa_g, xa_wq)


def _softmax_rows(s):
    e = jnp.exp(s - jnp.max(s, axis=-1, keepdims=True))
    return e / jnp.sum(e, axis=-1, keepdims=True)


def _xattn_body(q_ref, k_ref, v_ref, o_ref, *, bt, tq):
    scale = X_DH ** -0.5
    pairs = [(bi, h) for bi in range(bt) for h in range(X_HEADS)]
    hcol = lambda h: slice(h * X_DH, (h + 1) * X_DH)
    q = q_ref[...]
    scores = []
    for bi, h in pairs:
        qh = q[bi * tq:(bi + 1) * tq, hcol(h)].astype(BF16)
        scores.append(_dot_nt(qh, k_ref[bi, :, hcol(h)].astype(BF16)) * scale)
    outs = []
    for (bi, h), s in zip(pairs, scores):
        outs.append(_dot(_softmax_rows(s).astype(BF16), v_ref[bi, :, hcol(h)].astype(BF16)))
    rows = [jnp.concatenate(outs[bi * X_HEADS:(bi + 1) * X_HEADS], axis=1) for bi in range(bt)]
    o = rows[0] if bt == 1 else jnp.concatenate(rows, axis=0)
    o_ref[...] = o.astype(o_ref.dtype)


def _xattn_slot_head_body(q_ref, k_ref, v_ref, o_ref, *, bt, tq):
    scale = X_DH ** -0.5
    R = X_HEADS * tq
    NK = N_MEM * X_HEADS
    q_head = lax.broadcasted_iota(jnp.int32, (R, NK), 0) // tq
    k_head = lax.broadcasted_iota(jnp.int32, (R, NK), 1) % X_HEADS
    same = q_head == k_head
    q = q_ref[...]
    scores = []
    for bi in range(bt):
        qb = q[bi * tq:(bi + 1) * tq, :]
        q_rows = jnp.concatenate([qb[:, h * X_DH:(h + 1) * X_DH] for h in range(X_HEADS)], axis=0)
        k2 = k_ref[bi].reshape(NK, X_DH).astype(BF16)
        scores.append(jnp.where(same, _dot_nt(q_rows.astype(BF16), k2) * scale, -jnp.inf))
    rows = []
    for bi, s in enumerate(scores):
        v2 = v_ref[bi].reshape(NK, X_DH).astype(BF16)
        ob = _dot(_softmax_rows(s).astype(BF16), v2)
        rows.append(jnp.concatenate([ob[h * tq:(h + 1) * tq, :] for h in range(X_HEADS)], axis=1))
    o = rows[0] if bt == 1 else jnp.concatenate(rows, axis=0)
    o_ref[...] = o.astype(o_ref.dtype)


def xattn(q, mem_k, mem_v, lk, kcol, vcol, *, B, L, bt, tq):
    nq = L // tq
    R = bt * tq
    per_head = mem_k.ndim == 5
    if per_head:
        kv = lambda cb: pl.BlockSpec((None, bt, N_MEM, X_HEADS, X_DH),
                                     lambda i, j: (lk, i, 0, 0, 0))
    else:
        kv = lambda cb: pl.BlockSpec((None, bt, N_MEM, D_MODEL), lambda i, j: (lk, i, 0, cb))
    qo = pl.BlockSpec((R, D_MODEL), lambda i, j: (i * nq + j, 0))
    return pl.pallas_call(
        functools.partial(_xattn_slot_head_body if per_head else _xattn_body, bt=bt, tq=tq),
        grid=(B // bt, nq),
        in_specs=[qo, kv(kcol), kv(vcol)],
        out_specs=qo,
        out_shape=jax.ShapeDtypeStruct((B * L, D_MODEL), BF16),
        compiler_params=_cparams(2),
        name="xattn",
    )(q, mem_k, mem_v)


def _group_cfg(B, L):
    c = math.gcd(CHUNK, L)
    if L >= CHUNK:
        return dict(c=c, tl=min(512, L), m_bt=2, m_bu=2, g_bt=1, r_bt=2, r_bu=2, xa_bt=1,
                    xa_tq=min(512, L), idt=BF16)
    return dict(c=c, tl=L, m_bt=8, m_bu=4, g_bt=8, r_bt=8, r_bu=4, xa_bt=8, xa_tq=L, idt=F32)


def _layer(x, kv, state_in, l_in, prev, l, w, *, B, L, last_g):
    cfg = _group_cfg(B, L)
    c, tl, idt = cfg["c"], cfg["tl"], cfg["idt"]
    M = B * L
    conv0s, C0s, n0s, m0s, h0s, S0s = state_in
    C_prev, S_prev = prev if prev is not None else (None, None)

    x1, xn = ffn(x, w["f1_n"], w["f1_gu"], w["f1_d"], next_g=w["mix_n"], norm_dtype=BF16)

    conv2, conv_new = inproj_conv(xn, w["w_in_conv"], conv0s[l_in], w["conv_w"], w["conv_b"], B=B, L=L)
    mqkv = matmul(xn, w["w_in_mqkv"], idt, tm=1024, tn=1536)
    gates = matmul(xn, w["w_in_gates"], F32, tm=1024, tn=1536)
    small4 = matmul_t(w["w_in_small_t"], xn, tm=512).reshape(SMALL_PAD, B, L // c, c)
    conv3 = conv2.reshape(B, L, CONV_CH)
    mqkv3 = mqkv.reshape(B, L, 3 * D_MODEL)
    gates3 = gates.reshape(B, L, 3 * D_MODEL)

    hr, h1 = rglru(conv3, gates3, w["r_wa"], w["r_wx"], w["r_ba"], w["r_bx"], w["r_lam"],
                   h0s[l_in], tl=tl, out_dtype=idt)

    hm, C_out, n1, m1b = mlstm(mqkv3, gates3, small4, C0s, n0s, m0s, l_in, w["m_ib"], w["m_fb"],
                               w["m_norm"], C_prev, l, c=c, bt=cfg["m_bt"], bu=cfg["m_bu"], tl=tl,
                               out_dtype=idt)

    lm, qk, vb, kbg, qe, kd, eg = gdn_prep(conv3, small4, w["g_alog"], w["g_dtb"],
                                           c=c, bt=cfg["g_bt"], tl=tl, idt=idt)
    nchunks = M // c
    lt = lm.reshape(G_HEADS, nchunks, c, c).transpose(2, 3, 0, 1).reshape(c, c, G_HEADS * nchunks)
    tt = tri_inverse(lt)
    T = tt.reshape(c, c, G_HEADS, nchunks).transpose(2, 3, 0, 1).reshape(G_HEADS, B, L, c).astype(idt)
    hd, S_out = gdn_rec(T, qk, vb, kbg, qe, kd, eg, gates3, S0s, l_in, w["g_norm"], S_prev, l,
                        c=c, bt=cfg["r_bt"], bu=cfg["r_bu"], tl=tl, out_dtype=idt)

    x2, q = merge(x1, xn, hm.reshape(M, D_MODEL), hr.reshape(M, D_MODEL), hd.reshape(M, D_MODEL),
                  w["w_gate"], w["b_gate"], w["w_branch"], w["w_out"], w["xa_n"], w["xa_wq"])
    mem_k, mem_v, lk, kcol, vcol = kv
    o = xattn(q, mem_k, mem_v, lk, kcol, vcol, B=B, L=L, bt=cfg["xa_bt"], tq=cfg["xa_tq"])
    if last_g is None:
        x3, y = ffn(x2, w["f2_n"], w["f2_gu"], w["f2_d"], proj=(o, w["xa_wo"]))
    else:
        x3, y = ffn(x2, w["f2_n"], w["f2_gu"], w["f2_d"], proj=(o, w["xa_wo"]),
                    next_g=last_g, norm_dtype=F32)
    small_state = (conv_new, n1.reshape(B, M_HEADS, M_DH), m1b[:, :, 0, 0], h1)
    return x3, y, small_state, (C_out, S_out)


def _prep_weights(l, ffn1_norm, ffn1_w_gu, ffn1_w_down, mix_norm, w_in, conv_w, conv_b, m_igate_b,
                  m_fgate_b, m_norm, r_wa, r_ba, r_wx, r_bx, r_lambda, g_a_log, g_dt_bias, g_norm,
                  w_gate, b_gate, w_branch, w_out, xa_norm, xa_wq, xa_wo, ffn2_norm, ffn2_w_gu,
                  ffn2_w_down):
    row = lambda a: a[l].reshape(1, -1).astype(F32)
    b16 = lambda a: a[l].astype(BF16)
    small_t = jnp.zeros((SMALL_PAD, D_MODEL), BF16).at[:N_SMALL].set(
        w_in[l][:, OFF_SMALL:OFF_SMALL + N_SMALL].T.astype(BF16))
    return dict(
        f1_n=row(ffn1_norm), f1_gu=b16(ffn1_w_gu), f1_d=b16(ffn1_w_down), mix_n=row(mix_norm),
        w_in_conv=w_in[l][:, :CONV_CH].astype(BF16),
        w_in_mqkv=w_in[l][:, OFF_MQKV:OFF_GATES].astype(BF16),
        w_in_gates=w_in[l][:, OFF_GATES:OFF_SMALL].astype(BF16),
        w_in_small_t=small_t,
        conv_w=conv_w[l].astype(F32), conv_b=row(conv_b),
        m_ib=m_igate_b[l].astype(F32), m_fb=m_fgate_b[l].astype(F32), m_norm=row(m_norm),
        r_wa=b16(r_wa), r_wx=b16(r_wx), r_ba=row(r_ba), r_bx=row(r_bx), r_lam=row(r_lambda),
        g_alog=g_a_log[l].astype(F32), g_dtb=g_dt_bias[l].astype(F32), g_norm=row(g_norm),
        w_gate=b16(w_gate), b_gate=row(b_gate), w_branch=b16(w_branch), w_out=b16(w_out),
        xa_n=row(xa_norm), xa_wq=b16(xa_wq), xa_wo=b16(xa_wo),
        f2_n=row(ffn2_norm), f2_gu=b16(ffn2_w_gu), f2_d=b16(ffn2_w_down))


def _stacked_state(conv, C, n, m, h, S):
    Dp, B = C.shape[:2]
    return (conv, C, n.reshape(Dp, B, M_HEADS, 1, M_DH),
            jnp.broadcast_to(m[..., None, None], (Dp, B, M_HEADS, 1, LANE)), h, S)


def kernel(x_prompt, x_sample, cache_mem_k, cache_mem_v, state_conv, state_mlstm_C, state_mlstm_n, state_mlstm_m, state_rglru_h, state_delta_S, mem_prompt, ffn1_norm, ffn1_w_gu, ffn1_w_down, mix_norm, w_in, conv_w, conv_b, m_igate_b, m_fgate_b, m_norm, r_wa, r_ba, r_wx, r_bx, r_lambda, g_a_log, g_dt_bias, g_norm, w_gate, b_gate, w_branch, w_out, xa_norm, xa_wq, xa_wk, xa_wv, xa_wo, ffn2_norm, ffn2_w_gu, ffn2_w_down, final_norm):
    Bp, Lp, D = x_prompt.shape
    Bs, Ls, _ = x_sample.shape
    stacked = (ffn1_norm, ffn1_w_gu, ffn1_w_down, mix_norm, w_in, conv_w, conv_b, m_igate_b,
               m_fgate_b, m_norm, r_wa, r_ba, r_wx, r_bx, r_lambda, g_a_log, g_dt_bias, g_norm,
               w_gate, b_gate, w_branch, w_out, xa_norm, xa_wq, xa_wo, ffn2_norm, ffn2_w_gu,
               ffn2_w_down)
    fin = final_norm.reshape(1, D).astype(F32)
    memp = mem_prompt.reshape(Bp * N_MEM, D).astype(BF16)
    p_state = _stacked_state(jnp.zeros((1, Bp, CONV_W - 1, CONV_CH), F32),
                             jnp.zeros((1, Bp, M_HEADS, M_DH, M_DH), F32),
                             jnp.zeros((1, Bp, M_HEADS, M_DH), F32),
                             jnp.full((1, Bp, M_HEADS), M_INIT, F32),
                             jnp.zeros((1, Bp, R_WIDTH), F32),
                             jnp.zeros((1, Bp, G_HEADS, G_DH, G_DH), F32))
    s_state = _stacked_state(state_conv, state_mlstm_C, state_mlstm_n, state_mlstm_m,
                             state_rglru_h, state_delta_S)

    yp = x_prompt.reshape(Bp * Lp, D)
    ys = x_sample.reshape(Bs * Ls, D)
    p_mk, p_mv, p_small, s_small = [], [], [], []
    p_big = s_big = None
    outp = outs = None
    for l in range(DEPTH):
        w = _prep_weights(l, *stacked)
        last_g = fin if l == DEPTH - 1 else None
        wkv = jnp.concatenate([xa_wk[l], xa_wv[l]], axis=1).astype(BF16)
        mkv = matmul(memp, wkv, F32, tm=512, tn=1024)
        p_mk.append(mkv[:, :D].reshape(Bp, N_MEM, X_HEADS, X_DH))
        p_mv.append(mkv[:, D:].reshape(Bp, N_MEM, X_HEADS, X_DH))
        kv_p = (mkv.reshape(1, Bp, N_MEM, 2 * D), mkv.reshape(1, Bp, N_MEM, 2 * D), 0, 0, 1)
        yp, outp, sm, p_big = _layer(yp, kv_p, p_state, 0, p_big, l, w, B=Bp, L=Lp, last_g=last_g)
        p_small.append(sm)
        ys, outs, sm, s_big = _layer(ys, (cache_mem_k, cache_mem_v, l, 0, 0), s_state, l, s_big, l, w,
                                     B=Bs, L=Ls, last_g=last_g)
        s_small.append(sm)

    def assemble(small, big):
        conv, n, m, h = [jnp.stack([sm[i] for sm in small]) for i in range(4)]
        C, S = big
        return conv, C, n, m, h, S

    y_prompt = outp.reshape(Bp, Lp, D)
    y_sample = outs.reshape(Bs, Ls, D)
    return (y_prompt, y_sample, jnp.stack(p_mk), jnp.stack(p_mv),
            *assemble(p_small, p_big), *assemble(s_small, s_big))
```
